```python
import jax, jax.numpy as jnp
from jax import lax
import numpy as np


D_MODEL = 1024
BATCH = 16
SEQ = 2048
DEPTH = 2

HEAD_DIM = 64
NSA_HEADS = 8
NSA_KV_HEADS = 2
NSA_GROUP = NSA_HEADS // NSA_KV_HEADS
NSA_WIDTH = NSA_HEADS * HEAD_DIM
NSA_KV_WIDTH = NSA_KV_HEADS * HEAD_DIM
CMP_BLOCK = 32
CMP_STRIDE = 16
CMP_HIDDEN = 256
SLC_BLOCK = 64
SLC_TOPK = 8
WINDOW = 512
NSA_Q_CHUNK = 128
MOBA_HEADS = 8
MOBA_WIDTH = MOBA_HEADS * HEAD_DIM
MOBA_BLOCK = 256
MOBA_TOPK = 3
MOBA_Q_CHUNK = 32
D_FF = 2816
TOTAL_HEADS = NSA_HEADS + MOBA_HEADS
NEG_INF = -1e30
FORCE = 1e9
IN_SPLITS = (NSA_WIDTH, NSA_KV_WIDTH, NSA_KV_WIDTH, NSA_KV_WIDTH, NSA_KV_WIDTH, NSA_KV_WIDTH, NSA_KV_WIDTH,
             3 * NSA_HEADS, MOBA_WIDTH, MOBA_WIDTH, MOBA_WIDTH, D_MODEL, D_MODEL)
IN_COLS = sum(IN_SPLITS)

kernel_name = "hybrid_nsa_moba_macaron_alibi"


def rms_norm(x, g, eps=1e-6):
    xf = x.astype(jnp.float32)
    y = xf * lax.rsqrt(jnp.mean(xf * xf, axis=-1, keepdims=True) + eps)
    return (y * g.astype(jnp.float32)).astype(x.dtype)


def swiglu(x, w1, w3, w2):
    return (jax.nn.silu(x @ w1) * (x @ w3)) @ w2


def masked_softmax(s, mask):
    p = jax.nn.softmax(jnp.where(mask, s, NEG_INF), axis=-1)
    return jnp.where(mask, p, 0.0)


def alibi_slopes():
    s = jnp.asarray((2.0 ** (-8.0 * np.arange(1, TOTAL_HEADS + 1) / TOTAL_HEADS)).astype(np.float32))
    return s[0::2], s[1::2]


def split_heads(t, n_heads):
    b, s, _ = t.shape
    return t.reshape(b, s, n_heads, HEAD_DIM).transpose(0, 2, 1, 3)


def gather_blocks(blocks, idx):
    return jax.vmap(jax.vmap(lambda kb, ii: kb[ii]))(blocks, idx)


def compress_kv(t, pos, w1, w2):
    b, g, s, dh = t.shape
    r = CMP_BLOCK // CMP_STRIDE
    n_chunks = s // CMP_STRIDE
    n_cmp = n_chunks - r + 1
    ch = t.reshape(b, g, n_chunks, CMP_STRIDE, dh)
    blocks = jnp.concatenate([ch[:, :, j:j + n_cmp] for j in range(r)], axis=3)
    flat = (blocks + pos).reshape(b, g, n_cmp, CMP_BLOCK * dh)
    return jax.nn.gelu(flat @ w1) @ w2


def nsa_mixer(q, k_cmp, v_cmp, k_slc, v_slc, k_win, v_win, gates, g_qk, cmp_pos, cmp_w1, cmp_w2, slopes):
    b, s, _ = q.shape
    f32 = jnp.float32
    scale = HEAD_DIM ** -0.5
    qh = q.reshape(b, s, NSA_KV_HEADS, NSA_GROUP, HEAD_DIM).transpose(0, 2, 3, 1, 4)
    qh = rms_norm(qh, g_qk[0])
    slope = slopes.reshape(NSA_KV_HEADS, NSA_GROUP)[None, :, :, None, None]
    pos_t = jnp.arange(s, dtype=jnp.int32)

    kc = rms_norm(compress_kv(split_heads(k_cmp, NSA_KV_HEADS), cmp_pos[0], cmp_w1[0], cmp_w2[0]), g_qk[1])
    vc = compress_kv(split_heads(v_cmp, NSA_KV_HEADS), cmp_pos[1], cmp_w1[1], cmp_w2[1])
    n_cmp = kc.shape[2]
    c_start = jnp.arange(n_cmp, dtype=jnp.int32) * CMP_STRIDE
    c_valid = (c_start[None, :] + CMP_BLOCK - 1) <= pos_t[:, None]
    c_dist = pos_t[:, None].astype(f32) - (c_start.astype(f32) + (CMP_BLOCK - 1) / 2)[None, :]
    s_cmp = jnp.einsum('bgrsd,bgcd->bgrsc', qh, kc).astype(f32) * scale - slope * c_dist
    p_cmp = masked_softmax(s_cmp, c_valid)
    o_cmp = jnp.einsum('bgrsc,bgcd->bgrsd', p_cmp.astype(vc.dtype), vc)

    n_slc = s // SLC_BLOCK
    ci = np.arange(n_cmp)[:, None] * CMP_STRIDE
    sj = np.arange(n_slc)[None, :] * SLC_BLOCK
    overlap = np.clip(np.minimum(ci + CMP_BLOCK, sj + SLC_BLOCK) - np.maximum(ci, sj), 0, None)
    cmp_to_slc = jnp.asarray((overlap / CMP_BLOCK).astype(np.float32))
    imp = jnp.einsum('bgrsc,cj->bgsj', p_cmp, cmp_to_slc)
    blk = jnp.arange(n_slc, dtype=jnp.int32)[None, :]
    cur = (pos_t // SLC_BLOCK)[:, None]
    forced = (blk == 0) | (blk == cur) | (blk == cur - 1)
    sel_score = jnp.where(forced, FORCE, jnp.where(blk <= cur, imp, NEG_INF))
    n_sel = min(SLC_TOPK, n_slc)
    _, sel_idx = lax.top_k(sel_score, n_sel)

    ks_blocks = rms_norm(split_heads(k_slc, NSA_KV_HEADS), g_qk[2]).reshape(b, NSA_KV_HEADS, n_slc, SLC_BLOCK, HEAD_DIM)
    vs_blocks = split_heads(v_slc, NSA_KV_HEADS).reshape(b, NSA_KV_HEADS, n_slc, SLC_BLOCK, HEAD_DIM)
    pad = ((0, 0), (0, 0), (WINDOW, 0), (0, 0))
    kw_pad = jnp.pad(rms_norm(split_heads(k_win, NSA_KV_HEADS), g_qk[3]), pad)
    vw_pad = jnp.pad(split_heads(v_win, NSA_KV_HEADS), pad)
    n_tok = n_sel * SLC_BLOCK

    def chunk(c):
        start = c * NSA_Q_CHUNK
        t = start + jnp.arange(NSA_Q_CHUNK, dtype=jnp.int32)
        qc = lax.dynamic_slice_in_dim(qh, start, NSA_Q_CHUNK, axis=3)
        idx = lax.dynamic_slice_in_dim(sel_idx, start, NSA_Q_CHUNK, axis=2)
        ks = gather_blocks(ks_blocks, idx).reshape(b, NSA_KV_HEADS, NSA_Q_CHUNK, n_tok, HEAD_DIM)
        vs = gather_blocks(vs_blocks, idx).reshape(b, NSA_KV_HEADS, NSA_Q_CHUNK, n_tok, HEAD_DIM)
        s_pos = (idx[..., None] * SLC_BLOCK + jnp.arange(SLC_BLOCK, dtype=jnp.int32)).reshape(b, NSA_KV_HEADS, NSA_Q_CHUNK, n_tok)
        s_dist = (t[:, None] - s_pos)[:, :, None]
        sc = jnp.einsum('bgrqd,bgqkd->bgrqk', qc, ks).astype(f32) * scale - slope * s_dist.astype(f32)
        p = masked_softmax(sc, s_dist >= 0)
        o_s = jnp.einsum('bgrqk,bgqkd->bgrqd', p.astype(vs.dtype), vs)
        kwc = lax.dynamic_slice_in_dim(kw_pad, start, NSA_Q_CHUNK + WINDOW, axis=2)
        vwc = lax.dynamic_slice_in_dim(vw_pad, start, NSA_Q_CHUNK + WINDOW, axis=2)
        w_pos = start - WINDOW + jnp.arange(NSA_Q_CHUNK + WINDOW, dtype=jnp.int32)
        w_dist = t[:, None] - w_pos[None, :]
        w_mask = (w_dist >= 0) & (w_dist < WINDOW) & (w_pos[None, :] >= 0)
        sw = jnp.einsum('bgrqd,bgkd->bgrqk', qc, kwc).astype(f32) * scale - slope * w_dist.astype(f32)
        pw = masked_softmax(sw, w_mask)
        o_w = jnp.einsum('bgrqk,bgkd->bgrqd', pw.astype(vwc.dtype), vwc)
        return o_s, o_w

    o_slc, o_win = lax.map(chunk, jnp.arange(s // NSA_Q_CHUNK, dtype=jnp.int32))

    def unchunk(o):
        return o.transpose(1, 2, 3, 0, 4, 5).reshape(b, NSA_KV_HEADS, NSA_GROUP, s, HEAD_DIM)

    g = jax.nn.sigmoid(gates.astype(f32)).reshape(b, s, NSA_KV_HEADS, NSA_GROUP, 3).transpose(0, 2, 3, 1, 4).astype(q.dtype)
    o = g[..., 0:1] * o_cmp + g[..., 1:2] * unchunk(o_slc) + g[..., 2:3] * unchunk(o_win)
    return o.transpose(0, 3, 1, 2, 4).reshape(b, s, NSA_WIDTH)


def moba_mixer(q, k, v, g_qk, slopes):
    b, s, _ = q.shape
    f32 = jnp.float32
    scale = HEAD_DIM ** -0.5
    qh = rms_norm(split_heads(q, MOBA_HEADS), g_qk[0])
    kh = rms_norm(split_heads(k, MOBA_HEADS), g_qk[1])
    vh = split_heads(v, MOBA_HEADS)
    n_blk = -(-s // MOBA_BLOCK)
    pad = ((0, 0), (0, 0), (0, n_blk * MOBA_BLOCK - s), (0, 0))
    k_pad = jnp.pad(kh, pad)
    v_pad = jnp.pad(vh, pad)
    kb = k_pad.reshape(b, MOBA_HEADS, n_blk, MOBA_BLOCK, HEAD_DIM)
    vb = v_pad.reshape(b, MOBA_HEADS, n_blk, MOBA_BLOCK, HEAD_DIM)
    k_mean = jnp.mean(kb.astype(f32), axis=3).astype(kh.dtype)
    pos_t = jnp.arange(s, dtype=jnp.int32)
    cur = (pos_t // MOBA_BLOCK)[:, None]
    past = jnp.arange(n_blk, dtype=jnp.int32)[None, :] < cur
    gate = jnp.einsum('bhsd,bhnd->bhsn', qh, k_mean).astype(f32)
    n_top = min(MOBA_TOPK, n_blk)
    _, top_idx = lax.top_k(jnp.where(past, gate, NEG_INF), n_top)
    top_valid = top_idx < cur
    slope = slopes[None, :, None, None]
    n_tok = n_top * MOBA_BLOCK

    def chunk(c):
        start = c * MOBA_Q_CHUNK
        t = start + jnp.arange(MOBA_Q_CHUNK, dtype=jnp.int32)
        qc = lax.dynamic_slice_in_dim(qh, start, MOBA_Q_CHUNK, axis=2)
        idx = lax.dynamic_slice_in_dim(top_idx, start, MOBA_Q_CHUNK, axis=2)
        valid = lax.dynamic_slice_in_dim(top_valid, start, MOBA_Q_CHUNK, axis=2)
        kt = gather_blocks(kb, idx).reshape(b, MOBA_HEADS, MOBA_Q_CHUNK, n_tok, HEAD_DIM)
        vt = gather_blocks(vb, idx).reshape(b, MOBA_HEADS, MOBA_Q_CHUNK, n_tok, HEAD_DIM)
        t_pos = (idx[..., None] * MOBA_BLOCK + jnp.arange(MOBA_BLOCK, dtype=jnp.int32)).reshape(b, MOBA_HEADS, MOBA_Q_CHUNK, n_tok)
        t_dist = (t[:, None] - t_pos).astype(f32)
        t_mask = jnp.repeat(valid, MOBA_BLOCK, axis=-1)
        s_top = jnp.einsum('bhqd,bhqkd->bhqk', qc, kt).astype(f32) * scale - slope * t_dist
        own_start = (start // MOBA_BLOCK) * MOBA_BLOCK
        ko = lax.dynamic_slice_in_dim(k_pad, own_start, MOBA_BLOCK, axis=2)
        vo = lax.dynamic_slice_in_dim(v_pad, own_start, MOBA_BLOCK, axis=2)
        o_dist = t[:, None] - (own_start + jnp.arange(MOBA_BLOCK, dtype=jnp.int32))[None, :]
        s_own = jnp.einsum('bhqd,bhkd->bhqk', qc, ko).astype(f32) * scale - slope * o_dist.astype(f32)
        sc = jnp.concatenate([s_top, s_own], axis=-1)
        mask = jnp.concatenate([t_mask, jnp.broadcast_to(o_dist >= 0, (b, MOBA_HEADS, MOBA_Q_CHUNK, MOBA_BLOCK))], axis=-1)
        p = masked_softmax(sc, mask).astype(vh.dtype)
        return (jnp.einsum('bhqk,bhqkd->bhqd', p[..., :n_tok], vt)
                + jnp.einsum('bhqk,bhkd->bhqd', p[..., n_tok:], vo))

    o = lax.map(chunk, jnp.arange(s // MOBA_Q_CHUNK, dtype=jnp.int32))
    return o.transpose(1, 0, 3, 2, 4).reshape(b, s, MOBA_WIDTH)


def token_mixing(h, w_in, g_qk_nsa, g_qk_moba, cmp_pos, cmp_w1, cmp_w2, w_up_nsa, w_up_moba, w_out, slopes_nsa, slopes_moba):
    proj = h @ w_in
    cuts = np.cumsum(IN_SPLITS)[:-1].tolist()
    (q_n, kc, vc, ks, vs, kw, vw, g_n, q_m, k_m, v_m, gate_n, gate_m) = jnp.split(proj, cuts, axis=-1)
    o_n = nsa_mixer(q_n, kc, vc, ks, vs, kw, vw, g_n, g_qk_nsa, cmp_pos, cmp_w1, cmp_w2, slopes_nsa)
    o_m = moba_mixer(q_m, k_m, v_m, g_qk_moba, slopes_moba)
    y = jax.nn.sigmoid(gate_n) * (o_n @ w_up_nsa) + jax.nn.sigmoid(gate_m) * (o_m @ w_up_moba)
    return y @ w_out


def setup_inputs(seed: int = 0) -> dict:
    key = jax.random.key(seed)
    ks = jax.random.split(key, 15)
    f32 = jnp.float32

    def w(k, shape, fan_in):
        return jax.random.normal(k, shape, f32) * fan_in ** -0.5

    return {
        "x": jax.random.normal(ks[0], (BATCH, SEQ, D_MODEL), f32),
        "norm_g": 1.0 + 0.02 * jax.random.normal(ks[1], (DEPTH, 3, D_MODEL), f32),
        "ffn_w1": w(ks[2], (DEPTH, 2, D_MODEL, D_FF), D_MODEL),
        "ffn_w3": w(ks[3], (DEPTH, 2, D_MODEL, D_FF), D_MODEL),
        "ffn_w2": w(ks[4], (DEPTH, 2, D_FF, D_MODEL), D_FF),
        "w_in": w(ks[5], (DEPTH, D_MODEL, IN_COLS), D_MODEL),
        "g_qk_nsa": 1.0 + 0.02 * jax.random.normal(ks[6], (DEPTH, 4, HEAD_DIM), f32),
        "g_qk_moba": 1.0 + 0.02 * jax.random.normal(ks[7], (DEPTH, 2, HEAD_DIM), f32),
        "cmp_pos": 0.02 * jax.random.normal(ks[8], (DEPTH, 2, CMP_BLOCK, HEAD_DIM), f32),
        "cmp_w1": w(ks[9], (DEPTH, 2, CMP_BLOCK * HEAD_DIM, CMP_HIDDEN), CMP_BLOCK * HEAD_DIM),
        "cmp_w2": w(ks[10], (DEPTH, 2, CMP_HIDDEN, HEAD_DIM), CMP_HIDDEN),
        "w_up_nsa": w(ks[11], (DEPTH, NSA_WIDTH, D_MODEL), NSA_WIDTH),
        "w_up_moba": w(ks[12], (DEPTH, MOBA_WIDTH, D_MODEL), MOBA_WIDTH),
        "w_out": w(ks[13], (DEPTH, D_MODEL, D_MODEL), D_MODEL),
    }


def reference(x, norm_g, ffn_w1, ffn_w3, ffn_w2, w_in, g_qk_nsa, g_qk_moba, cmp_pos, cmp_w1, cmp_w2, w_up_nsa, w_up_moba, w_out):
    slopes_nsa, slopes_moba = alibi_slopes()
    for l in range(DEPTH):
        x = x + 0.5 * swiglu(rms_norm(x, norm_g[l, 0]), ffn_w1[l, 0], ffn_w3[l, 0], ffn_w2[l, 0])
        x = x + token_mixing(rms_norm(x, norm_g[l, 1]), w_in[l], g_qk_nsa[l], g_qk_moba[l], cmp_pos[l],
                             cmp_w1[l], cmp_w2[l], w_up_nsa[l], w_up_moba[l], w_out[l], slopes_nsa, slopes_moba)
        x = x + 0.5 * swiglu(rms_norm(x, norm_g[l, 2]), ffn_w1[l, 1], ffn_w3[l, 1], ffn_w2[l, 1])
    return x
```

```python
import functools

import numpy as np
import jax
import jax.numpy as jnp
from jax import lax
from jax.experimental import pallas as pl
from jax.experimental.pallas import tpu as pltpu

F32 = jnp.float32
BF16 = jnp.bfloat16

HEAD_DIM = 64
NSA_HEADS = 8
NSA_KV_HEADS = 2
NSA_GROUP = NSA_HEADS // NSA_KV_HEADS
CMP_BLOCK = 32
CMP_STRIDE = 16
SLC_BLOCK = 64
SLC_TOPK = 8
WINDOW = 512
MOBA_HEADS = 8
MOBA_BLOCK = 256
MOBA_TOPK = 3
TOTAL_HEADS = NSA_HEADS + MOBA_HEADS
NEG_INF = -1e30
FORCE = 1e9
RMS_EPS = 1e-6
SCALE = HEAD_DIM ** -0.5

VMEM_LIMIT_BYTES = 56 * 1024 * 1024

FFN_TM = 512
FFN_FC = 256
NSA_TQ = 128
NSA_TK = 256
MOBA_TQ = MOBA_BLOCK


def _rms(x, g):
    ms = jnp.mean(x * x, axis=-1, keepdims=True)
    return x * lax.rsqrt(ms + RMS_EPS) * g


def _dot(a, b):
    return jnp.dot(a, b, preferred_element_type=F32)


def _dot_nt(a, b):
    return lax.dot_general(a, b, (((1,), (1,)), ((), ())), preferred_element_type=F32)


def _split(a):
    hi = a.astype(BF16)
    lo = (a - hi.astype(F32)).astype(BF16)
    return hi, lo


def _dot_nt_precise(a, b):
    ah, al = _split(a)
    bh, bl = _split(b)
    return _dot_nt(ah, bh) + (_dot_nt(ah, bl) + _dot_nt(al, bh))


def _iota(shape, dim):
    return lax.broadcasted_iota(jnp.int32, shape, dim)


def _transpose_01(m_t, n):
    eye = (_iota((n, n), 0) == _iota((n, n), 1)).astype(BF16)
    return _dot_nt(eye, m_t.astype(BF16))


def _topk_mask_t(score, k):
    rows = score.shape[0]
    ridx = _iota(score.shape, 0)
    rank = jnp.zeros(score.shape, jnp.int32)
    for j in range(rows):
        sj = score[j:j + 1, :]
        beats = (sj > score) | ((sj == score) & (ridx > j))
        rank = rank + beats.astype(jnp.int32)
    return rank < k


def _ffn_kernel(x_ref, g_ref, w1_ref, w3_ref, w2_ref, o_ref):
    x = x_ref[...]
    hb = _rms(x, g_ref[...]).astype(BF16)
    d_ff = w1_ref.shape[1]
    acc = jnp.zeros(x.shape, F32)
    for c in range(d_ff // FFN_FC):
        sl = slice(c * FFN_FC, (c + 1) * FFN_FC)
        a = _dot(hb, w1_ref[:, sl])
        b = _dot(hb, w3_ref[:, sl])
        u = (a * jax.nn.sigmoid(a) * b).astype(BF16)
        acc = acc + _dot(u, w2_ref[sl, :])
    o_ref[...] = x + 0.5 * acc


def _const_spec(shape):
    nd = len(shape)
    return pl.BlockSpec(shape, lambda *_: (0,) * nd, pipeline_mode=pl.Buffered(1))


def _ffn(x, g, w1, w3, w2):
    n, d = x.shape
    d_ff = w1.shape[1]
    assert n % FFN_TM == 0 and d_ff % FFN_FC == 0
    return pl.pallas_call(
        _ffn_kernel,
        grid=(n // FFN_TM,),
        in_specs=[
            pl.BlockSpec((FFN_TM, d), lambda i: (i, 0)),
            _const_spec((1, d)),
            _const_spec((d, d_ff)),
            _const_spec((d, d_ff)),
            _const_spec((d_ff, d)),
        ],
        out_specs=pl.BlockSpec((FFN_TM, d), lambda i: (i, 0)),
        out_shape=jax.ShapeDtypeStruct((n, d), F32),
        compiler_params=pltpu.CompilerParams(
            dimension_semantics=("arbitrary",), vmem_limit_bytes=VMEM_LIMIT_BYTES),
        name="ffn",
    )(x, g.reshape(1, d), w1, w3, w2)


_OFF_QN = 0
_OFF_KV = 512
_OFF_QM = 1280
_OFF_KM = 1792
_OFF_VM = 2304
_OFF_GATE_N = 2816
_OFF_GATE_M = 3840
_OFF_GN = 4864
_IN_COLS_PACKED = 5120


def _inproj_kernel(x_ref, g_ref, w_ref, qn_ref, kv_ref, qm_ref, km_ref, vm_ref,
                   gate_n_ref, gate_m_ref, gn_ref):
    hb = _rms(x_ref[0], g_ref[...]).astype(BF16)

    def proj(off, width):
        return _dot(hb, w_ref[:, off:off + width])

    def emit_heads(ref, p, n_heads, lead=()):
        for h in range(n_heads):
            ref[(0,) + lead + (h,)] = p[:, h * HEAD_DIM:(h + 1) * HEAD_DIM]

    emit_heads(qn_ref, proj(_OFF_QN, 512), NSA_HEADS)
    p = proj(_OFF_KV, 768)
    for a in range(6):
        for g in range(NSA_KV_HEADS):
            c0 = (a * NSA_KV_HEADS + g) * HEAD_DIM
            kv_ref[0, a, g] = p[:, c0:c0 + HEAD_DIM]
    emit_heads(qm_ref, proj(_OFF_QM, 512), MOBA_HEADS)
    emit_heads(km_ref, proj(_OFF_KM, 512), MOBA_HEADS)
    emit_heads(vm_ref, proj(_OFF_VM, 512), MOBA_HEADS)
    gate_n_ref[0] = proj(_OFF_GATE_N, 1024)
    gate_m_ref[0] = proj(_OFF_GATE_M, 1024)
    gn_ref[0] = proj(_OFF_GN, 256)


def _inproj(x, g, w):
    b, s, d = x.shape
    tm = FFN_TM
    assert s % tm == 0
    hs = lambda n: jax.ShapeDtypeStruct((b, n, s, HEAD_DIM), F32)
    hspec = lambda n: pl.BlockSpec((1, n, tm, HEAD_DIM), lambda bi, i: (bi, 0, i, 0))
    return pl.pallas_call(
        _inproj_kernel,
        grid=(b, s // tm),
        in_specs=[
            pl.BlockSpec((1, tm, d), lambda bi, i: (bi, i, 0)),
            _const_spec((1, d)),
            _const_spec((d, _IN_COLS_PACKED)),
        ],
        out_specs=[
            hspec(NSA_HEADS),
            pl.BlockSpec((1, 6, NSA_KV_HEADS, tm, HEAD_DIM), lambda bi, i: (bi, 0, 0, i, 0)),
            hspec(MOBA_HEADS), hspec(MOBA_HEADS), hspec(MOBA_HEADS),
            pl.BlockSpec((1, tm, 1024), lambda bi, i: (bi, i, 0)),
            pl.BlockSpec((1, tm, 1024), lambda bi, i: (bi, i, 0)),
            pl.BlockSpec((1, tm, 256), lambda bi, i: (bi, i, 0)),
        ],
        out_shape=[
            hs(NSA_HEADS),
            jax.ShapeDtypeStruct((b, 6, NSA_KV_HEADS, s, HEAD_DIM), F32),
            hs(MOBA_HEADS), hs(MOBA_HEADS), hs(MOBA_HEADS),
            jax.ShapeDtypeStruct((b, s, 1024), F32),
            jax.ShapeDtypeStruct((b, s, 1024), F32),
            jax.ShapeDtypeStruct((b, s, 256), F32),
        ],
        compiler_params=pltpu.CompilerParams(
            dimension_semantics=("arbitrary", "arbitrary"), vmem_limit_bytes=VMEM_LIMIT_BYTES),
        name="inproj",
    )(x, g.reshape(1, d), w)


def _compress_kernel(kc_ref, vc_ref, pos_ref, w1_ref, w2_ref, gk_ref, kco_ref, vco_ref):
    half = CMP_STRIDE * HEAD_DIM
    for a, (src, dst) in enumerate(((kc_ref, kco_ref), (vc_ref, vco_ref))):
        ch = src[0, 0, 0]
        n_chunks = ch.shape[0]
        lo = (ch + pos_ref[a, 0:1, :]).astype(BF16)
        hi = (ch + pos_ref[a, 1:2, :]).astype(BF16)
        first = _dot(lo, w1_ref[a, :half, :])
        second = _dot(hi, w1_ref[a, half:, :])
        hid = first + pltpu.roll(second, n_chunks - 1, 0)
        y = _dot(jax.nn.gelu(hid).astype(BF16), w2_ref[a])
        if a == 0:
            y = _rms(y, gk_ref[...])
        dst[0, 0] = jnp.where(_iota(y.shape, 0) < n_chunks - 1, y, 0.0)


def _compress(kv, pos, w1, w2, gk):
    b, _, g, s, _ = kv.shape
    n_chunks = s // CMP_STRIDE
    half = CMP_STRIDE * HEAD_DIM
    kv_chunks = kv.reshape(b, 6, g, n_chunks, half)
    out = jax.ShapeDtypeStruct((b, g, n_chunks, HEAD_DIM), F32)
    src = lambda a: pl.BlockSpec((1, 1, 1, n_chunks, half), lambda bi, gi: (bi, a, gi, 0, 0))
    ospec = pl.BlockSpec((1, 1, n_chunks, HEAD_DIM), lambda bi, gi: (bi, gi, 0, 0))
    return pl.pallas_call(
        _compress_kernel,
        grid=(b, g),
        in_specs=[src(0), src(1), _const_spec(pos.shape), _const_spec(w1.shape),
                  _const_spec(w2.shape), _const_spec((1, HEAD_DIM))],
        out_specs=[ospec, ospec],
        out_shape=[out, out],
        compiler_params=pltpu.CompilerParams(
            dimension_semantics=("arbitrary", "arbitrary"), vmem_limit_bytes=VMEM_LIMIT_BYTES),
        name="compress",
    )(kv_chunks, kv_chunks, pos, w1, w2, gk.reshape(1, HEAD_DIM))


def _masked_softmax_rows(s, valid):
    sm = jnp.where(valid, s, NEG_INF)
    m = jnp.max(sm, axis=1, keepdims=True)
    p = jnp.where(valid, jnp.exp(sm - m), 0.0)
    l = jnp.sum(p, axis=1, keepdims=True)
    return p, jnp.where(l > 0.0, 1.0 / l, 0.0)


def _nsa_kernel(slopes_ref, q_ref, kc_ref, vc_ref, ks_ref, vs_ref, kw_ref, vw_ref,
                gn_ref, gqk_ref, c2s_ref, o_ref, ksn_s, vsb_s, kwn_s, vwb_s):
    gi = pl.program_id(1)
    qi = pl.program_id(2)
    tq = NSA_TQ
    r_heads = NSA_GROUP
    seq = ks_ref.shape[3]

    @pl.when(qi == 0)
    def _():
        ksn_s[...] = _rms(ks_ref[0, 0, 0], gqk_ref[2:3, :]).astype(BF16)
        vsb_s[...] = vs_ref[0, 0, 0].astype(BF16)
        kwn_s[...] = _rms(kw_ref[0, 0, 0], gqk_ref[3:4, :]).astype(BF16)
        vwb_s[...] = vw_ref[0, 0, 0].astype(BF16)

    start = qi * tq
    slopes = [slopes_ref[gi, r] for r in range(r_heads)]
    qn = _rms(q_ref[0], gqk_ref[0:1, :]) * SCALE
    qs = qn.reshape(r_heads * tq, HEAD_DIM)
    qb = qs.astype(BF16)
    t_col = start + _iota((tq, 1), 0)

    def per_head(x, r):
        return x[r * tq:(r + 1) * tq]

    kc = kc_ref[0, 0]
    vcb = vc_ref[0, 0].astype(BF16)
    n_c = kc.shape[0]
    s_c = _dot_nt_precise(qs, kc)
    c_start = _iota((tq, n_c), 1) * CMP_STRIDE
    c_valid = (c_start + (CMP_BLOCK - 1)) <= t_col
    c_dist = t_col.astype(F32) - (c_start.astype(F32) + (CMP_BLOCK - 1) / 2)
    o_cmp = []
    p_sum = jnp.zeros((tq, n_c), F32)
    for r in range(r_heads):
        p, inv = _masked_softmax_rows(per_head(s_c, r) - slopes[r] * c_dist, c_valid)
        p = p * inv
        p_sum = p_sum + p
        o_cmp.append(_dot(p.astype(BF16), vcb))

    n_slc = c2s_ref.shape[0]
    ph, pl_ = _split(p_sum)
    c2s = c2s_ref[...].astype(BF16)
    imp_t = _dot_nt(c2s, ph) + _dot_nt(c2s, pl_)
    blk = _iota((n_slc, tq), 0)
    cur = lax.shift_right_logical(start + _iota((n_slc, tq), 1), 6)
    forced = (blk == 0) | (blk == cur) | (blk == cur - 1)
    sel_score = jnp.where(forced, FORCE, jnp.where(blk <= cur, imp_t, NEG_INF))
    sel_t = _topk_mask_t(sel_score, min(SLC_TOPK, n_slc)).astype(F32)
    sel_q = _transpose_01(sel_t, tq).astype(BF16)

    tk = NSA_TK
    n_chunks = (start + tq + tk - 1) // tk

    def slc_body(c, carry):
        m, l, acc = carry
        k0 = pl.multiple_of(c * tk, tk)
        k = ksn_s[pl.ds(k0, tk), :]
        v = vsb_s[pl.ds(k0, tk), :]
        s = _dot_nt(qb, k)
        pos = k0 + _iota((tq, tk), 1)
        dist = t_col - pos
        expand = (lax.shift_right_logical(k0 + _iota((n_slc, tk), 1), 6)
                  == _iota((n_slc, tk), 0)).astype(BF16)
        chosen = _dot(sel_q, expand)
        valid = (chosen > 0.5) & (dist >= 0)
        distf = dist.astype(F32)
        sm = jnp.concatenate(
            [jnp.where(valid, per_head(s, r) - slopes[r] * distf, NEG_INF) for r in range(r_heads)],
            axis=0)
        m_new = jnp.maximum(m, jnp.max(sm, axis=1, keepdims=True))
        alpha = jnp.exp(m - m_new)
        p = jnp.exp(sm - m_new)
        l_new = alpha * l + jnp.sum(p, axis=1, keepdims=True)
        acc_new = alpha * acc + _dot(p.astype(BF16), v)
        return m_new, l_new, acc_new

    rows = r_heads * tq
    m0 = jnp.full((rows, 1), NEG_INF, F32)
    l0 = jnp.zeros((rows, 1), F32)
    a0 = jnp.zeros((rows, HEAD_DIM), F32)
    _, l_s, acc_s = lax.fori_loop(0, n_chunks, slc_body, (m0, l0, a0))
    o_slc = acc_s * (1.0 / l_s)

    wk = tq + WINDOW
    ws = pl.multiple_of(jnp.maximum(start - WINDOW, 0), tq)
    kw = kwn_s[pl.ds(ws, wk), :]
    vw = vwb_s[pl.ds(ws, wk), :]
    s_w = _dot_nt(qb, kw)
    w_dist = t_col - (ws + _iota((tq, wk), 1))
    w_valid = (w_dist >= 0) & (w_dist < WINDOW)
    w_distf = w_dist.astype(F32)
    o_win = []
    for r in range(r_heads):
        p, inv = _masked_softmax_rows(per_head(s_w, r) - slopes[r] * w_distf, w_valid)
        o_win.append(_dot(p.astype(BF16), vw) * inv)

    gate = jax.nn.sigmoid(gn_ref[0])
    for r in range(r_heads):
        o = (gate[:, 3 * r:3 * r + 1] * o_cmp[r]
             + gate[:, 3 * r + 1:3 * r + 2] * per_head(o_slc, r)
             + gate[:, 3 * r + 2:3 * r + 3] * o_win[r])
        o_ref[0, :, r * HEAD_DIM:(r + 1) * HEAD_DIM] = o.astype(o_ref.dtype)


def _nsa(slopes, qn, kc, vc, kv, gn, gqk, c2s):
    b, _, s, _ = qn.shape
    g = NSA_KV_HEADS
    tq = NSA_TQ
    n_c = kc.shape[2]
    assert s % tq == 0 and s % NSA_TK == 0 and s >= tq + WINDOW
    kvspec = lambda a: pl.BlockSpec((1, 1, 1, s, HEAD_DIM), lambda bi, gi, qi: (bi, a, gi, 0, 0))
    cspec = pl.BlockSpec((1, 1, n_c, HEAD_DIM), lambda bi, gi, qi: (bi, gi, 0, 0))
    return pl.pallas_call(
        _nsa_kernel,
        grid=(b, g, s // tq),
        in_specs=[
            pl.BlockSpec(memory_space=pltpu.SMEM),
            pl.BlockSpec((1, NSA_GROUP, tq, HEAD_DIM), lambda bi, gi, qi: (bi, gi, qi, 0)),
            cspec, cspec,
            kvspec(2), kvspec(3), kvspec(4), kvspec(5),
            pl.BlockSpec((1, tq, 128), lambda bi, gi, qi: (bi, qi, gi)),
            pl.BlockSpec(gqk.shape, lambda bi, gi, qi: (0, 0)),
            pl.BlockSpec(c2s.shape, lambda bi, gi, qi: (0, 0)),
        ],
        out_specs=pl.BlockSpec((1, tq, NSA_GROUP * HEAD_DIM), lambda bi, gi, qi: (bi, qi, gi)),
        out_shape=jax.ShapeDtypeStruct((b, s, NSA_HEADS * HEAD_DIM), BF16),
        scratch_shapes=[pltpu.VMEM((s, HEAD_DIM), BF16)] * 4,
        compiler_params=pltpu.CompilerParams(
            dimension_semantics=("arbitrary", "arbitrary", "arbitrary"),
            vmem_limit_bytes=VMEM_LIMIT_BYTES),
        name="nsa",
    )(slopes, qn, kc, vc, kv, kv, kv, kv, gn, gqk, c2s)


def _moba_kernel(slopes_ref, q_ref, k_ref, v_ref, gqk_ref, o_ref, kn_s, vb_s, kmean_s):
    hp = pl.program_id(1)
    qi = pl.program_id(2)
    tq = MOBA_TQ
    blk = MOBA_BLOCK
    seq = k_ref.shape[2]
    n_blk = seq // blk
    heads = q_ref.shape[1]

    @pl.when(qi == 0)
    def _():
        for hh in range(heads):
            kn = _rms(k_ref[0, hh], gqk_ref[1:2, :])
            kn_s[hh] = kn.astype(BF16)
            vb_s[hh] = v_ref[0, hh].astype(BF16)
            for n in range(n_blk):
                kmean_s[hh, n:n + 1, :] = jnp.mean(kn[n * blk:(n + 1) * blk], axis=0, keepdims=True)

    start = qi * tq
    t_col = start + _iota((tq, 1), 0)
    for hh in range(heads):
        slope = slopes_ref[hp * heads + hh]
        qn = _rms(q_ref[0, hh], gqk_ref[0:1, :])
        qb = (qn * SCALE).astype(BF16)

        gate_t = _dot_nt_precise(kmean_s[hh], qn)
        past = _iota((n_blk, tq), 0) < qi
        top = _topk_mask_t(jnp.where(past, gate_t, NEG_INF), min(MOBA_TOPK, n_blk))
        sel_t = (top & past).astype(F32)
        sel_q = _transpose_01(sel_t, tq)

        k0 = pl.multiple_of(start, blk)
        s = _dot_nt(qb, kn_s[hh, pl.ds(k0, blk), :])
        dist = _iota((tq, blk), 0) - _iota((tq, blk), 1)
        sm = jnp.where(dist >= 0, s - slope * dist.astype(F32), NEG_INF)
        m0 = jnp.max(sm, axis=1, keepdims=True)
        p = jnp.exp(sm - m0)
        l0 = jnp.sum(p, axis=1, keepdims=True)
        a0 = _dot(p.astype(BF16), vb_s[hh, pl.ds(k0, blk), :])

        def past_body(n, carry):
            m, l, acc = carry
            kb0 = pl.multiple_of(n * blk, blk)
            s = _dot_nt(qb, kn_s[hh, pl.ds(kb0, blk), :])
            distf = (t_col - (kb0 + _iota((tq, blk), 1))).astype(F32)
            chosen = jnp.sum(jnp.where(_iota((tq, n_blk), 1) == n, sel_q, 0.0), axis=1, keepdims=True)
            sm = jnp.where(chosen > 0.5, s - slope * distf, NEG_INF)
            m_new = jnp.maximum(m, jnp.max(sm, axis=1, keepdims=True))
            alpha = jnp.exp(m - m_new)
            p = jnp.exp(sm - m_new)
            l_new = alpha * l + jnp.sum(p, axis=1, keepdims=True)
            acc_new = alpha * acc + _dot(p.astype(BF16), vb_s[hh, pl.ds(kb0, blk), :])
            return m_new, l_new, acc_new

        _, l_f, acc_f = lax.fori_loop(0, qi, past_body, (m0, l0, a0))
        o = acc_f * (1.0 / l_f)
        o_ref[0, :, hh * HEAD_DIM:(hh + 1) * HEAD_DIM] = o.astype(o_ref.dtype)


def _moba(slopes, q, k, v, gqk):
    b, h, s, _ = q.shape
    tq = MOBA_TQ
    hpp = 2
    assert s % MOBA_BLOCK == 0 and h % hpp == 0
    kvspec = pl.BlockSpec((1, hpp, s, HEAD_DIM), lambda bi, hi, qi: (bi, hi, 0, 0))
    return pl.pallas_call(
        _moba_kernel,
        grid=(b, h // hpp, s // tq),
        in_specs=[
            pl.BlockSpec(memory_space=pltpu.SMEM),
            pl.BlockSpec((1, hpp, tq, HEAD_DIM), lambda bi, hi, qi: (bi, hi, qi, 0)),
            kvspec, kvspec,
            pl.BlockSpec(gqk.shape, lambda bi, hi, qi: (0, 0)),
        ],
        out_specs=pl.BlockSpec((1, tq, hpp * HEAD_DIM), lambda bi, hi, qi: (bi, qi, hi)),
        out_shape=jax.ShapeDtypeStruct((b, s, h * HEAD_DIM), BF16),
        scratch_shapes=[
            pltpu.VMEM((hpp, s, HEAD_DIM), BF16),
            pltpu.VMEM((hpp, s, HEAD_DIM), BF16),
            pltpu.VMEM((hpp, s // MOBA_BLOCK, HEAD_DIM), F32),
        ],
        compiler_params=pltpu.CompilerParams(
            dimension_semantics=("arbitrary", "arbitrary", "arbitrary"),
            vmem_limit_bytes=VMEM_LIMIT_BYTES),
        name="moba",
    )(slopes, q, k, v, gqk)


def _outproj_kernel(x_ref, on_ref, om_ref, gate_n_ref, gate_m_ref, wun_ref, wum_ref, wo_ref, o_ref):
    y = (jax.nn.sigmoid(gate_n_ref[...]) * _dot(on_ref[...], wun_ref[...])
         + jax.nn.sigmoid(gate_m_ref[...]) * _dot(om_ref[...], wum_ref[...]))
    o_ref[...] = x_ref[...] + _dot(y.astype(BF16), wo_ref[...])


def _outproj(x, o_n, o_m, gate_n, gate_m, wun, wum, wo):
    n, d = x.shape
    tm = FFN_TM
    row = lambda w: pl.BlockSpec((tm, w), lambda i: (i, 0))
    return pl.pallas_call(
        _outproj_kernel,
        grid=(n // tm,),
        in_specs=[row(d), row(o_n.shape[1]), row(o_m.shape[1]), row(d), row(d),
                  _const_spec(wun.shape), _const_spec(wum.shape), _const_spec(wo.shape)],
        out_specs=row(d),
        out_shape=jax.ShapeDtypeStruct((n, d), F32),
        compiler_params=pltpu.CompilerParams(
            dimension_semantics=("arbitrary",), vmem_limit_bytes=VMEM_LIMIT_BYTES),
        name="outproj",
    )(x, o_n, o_m, gate_n, gate_m, wun, wum, wo)


def _pack_w_in(w):
    d = w.shape[0]
    gn = w[:, 1280:1304].reshape(d, NSA_KV_HEADS, NSA_GROUP * 3)
    gn = jnp.pad(gn, ((0, 0), (0, 0), (0, 128 - NSA_GROUP * 3))).reshape(d, NSA_KV_HEADS * 128)
    return jnp.concatenate([w[:, :1280], w[:, 1304:], gn], axis=1).astype(BF16)


def _alibi_slopes():
    s = (2.0 ** (-8.0 * np.arange(1, TOTAL_HEADS + 1) / TOTAL_HEADS)).astype(np.float32)
    return (jnp.asarray(s[0::2].reshape(NSA_KV_HEADS, NSA_GROUP)), jnp.asarray(s[1::2]))


def _cmp_to_slc_t(n_chunks, n_slc):
    ci = np.arange(n_chunks)[None, :] * CMP_STRIDE
    sj = np.arange(n_slc)[:, None] * SLC_BLOCK
    overlap = np.clip(np.minimum(ci + CMP_BLOCK, sj + SLC_BLOCK) - np.maximum(ci, sj), 0, None)
    m = (overlap / CMP_BLOCK).astype(np.float32)
    m[:, n_chunks - 1] = 0.0
    return jnp.asarray(m)


def kernel(x, norm_g, ffn_w1, ffn_w3, ffn_w2, w_in, g_qk_nsa, g_qk_moba, cmp_pos, cmp_w1, cmp_w2,
           w_up_nsa, w_up_moba, w_out):
    b, s, d = x.shape
    depth = norm_g.shape[0]
    slopes_nsa, slopes_moba = _alibi_slopes()
    c2s_t = _cmp_to_slc_t(s // CMP_STRIDE, s // SLC_BLOCK)
    xf = x.reshape(b * s, d)
    for l in range(depth):
        xf = _ffn(xf, norm_g[l, 0], ffn_w1[l, 0].astype(BF16), ffn_w3[l, 0].astype(BF16),
                  ffn_w2[l, 0].astype(BF16))
        qn, kv, qm, km, vm, gate_n, gate_m, gn = _inproj(
            xf.reshape(b, s, d), norm_g[l, 1], _pack_w_in(w_in[l]))
        pos = cmp_pos[l].reshape(2, 2, CMP_STRIDE * HEAD_DIM)
        kc, vc = _compress(kv, pos, cmp_w1[l].astype(BF16), cmp_w2[l].astype(BF16), g_qk_nsa[l, 1])
        o_n = _nsa(slopes_nsa, qn, kc, vc, kv, gn, g_qk_nsa[l], c2s_t)
        o_m = _moba(slopes_moba, qm, km, vm, g_qk_moba[l])
        xf = _outproj(xf, o_n.reshape(b * s, -1), o_m.reshape(b * s, -1),
                      gate_n.reshape(b * s, -1), gate_m.reshape(b * s, -1),
                      w_up_nsa[l].astype(BF16), w_up_moba[l].astype(BF16), w_out[l].astype(BF16))
        xf = _ffn(xf, norm_g[l, 2], ffn_w1[l, 1].astype(BF16), ffn_w3[l, 1].astype(BF16),
                  ffn_w2[l, 1].astype(BF16))
    return xf.reshape(b, s, d)
```

```python
import numpy as np
import jax
import jax.numpy as jnp
from jax import lax
from jax.experimental import pallas as pl
from jax.experimental.pallas import tpu as pltpu

F32 = jnp.float32
BF16 = jnp.bfloat16

HEAD_DIM = 64
NSA_HEADS = 8
NSA_KV_HEADS = 2
NSA_GROUP = NSA_HEADS // NSA_KV_HEADS
CMP_BLOCK = 32
CMP_STRIDE = 16
SLC_BLOCK = 64
SLC_TOPK = 8
WINDOW = 512
MOBA_HEADS = 8
MOBA_BLOCK = 256
MOBA_TOPK = 3
TOTAL_HEADS = NSA_HEADS + MOBA_HEADS
NEG_INF = -1e30
FORCE = 1e9
RMS_EPS = 1e-6
SCALE = HEAD_DIM ** -0.5
LOG2E = 1.4426950408889634

VMEM_LIMIT_BYTES = 56 * 1024 * 1024
LANES = 128

FFN_TM = 512
FFN_FC = 256
NSA_TQ = 128
NSA_KEY_BUCKET = 256
MOBA_TQ = MOBA_BLOCK
KEY_CHUNK = 256

X_BLOCKS = 32
N_PIECES = 4
X_HI = X_BLOCKS
X_LO = X_HI + N_PIECES
X_PAD = X_LO + N_PIECES
MASK_BIG = 2.0 ** 20


def _rms(x, g):
    ms = jnp.mean(x * x, axis=-1, keepdims=True)
    return x * lax.rsqrt(ms + RMS_EPS) * g


def _rms_t(xt, g_col):
    ms = jnp.mean(xt * xt, axis=-2, keepdims=True)
    return xt * lax.rsqrt(ms + RMS_EPS) * g_col


def _dot(a, b):
    return jnp.dot(a, b, preferred_element_type=F32)


def _dot_nt(a, b):
    return lax.dot_general(a, b, (((1,), (1,)), ((), ())), preferred_element_type=F32)


def _split(a):
    hi = a.astype(BF16)
    lo = (a - hi.astype(F32)).astype(BF16)
    return hi, lo


def _dot_precise(a, b):
    ah, al = _split(a)
    bh, bl = _split(b)
    return _dot(ah, bh) + (_dot(ah, bl) + _dot(al, bh))


def _iota(shape, dim):
    return lax.broadcasted_iota(jnp.int32, shape, dim)


def _topk_mask_t(score, k):
    rows = score.shape[0]
    ridx = _iota(score.shape, 0)
    rank = jnp.zeros(score.shape, jnp.int32)
    for j in range(rows):
        sj = score[j:j + 1, :]
        beats = (sj > score) | ((sj == score) & (ridx > j))
        rank = rank + beats.astype(jnp.int32)
    return rank < k


def _key_extras(n_rows, block_shift, with_block_id):
    pos = _iota((n_rows, HEAD_DIM), 0)
    lane = _iota((n_rows, HEAD_DIM), 1)
    hi = (lax.shift_right_logical(pos, 6) * 64).astype(F32)
    lo = (pos & 63).astype(F32)
    x = jnp.where((lane >= X_HI) & (lane < X_LO), hi,
                  jnp.where((lane >= X_LO) & (lane < X_PAD), lo, 0.0))
    if with_block_id:
        x = jnp.where(lax.shift_right_logical(pos, block_shift) == lane, 1.0, x)
    return x


def _query_extras_t(not_selected_t, pieces, n):
    parts = []
    used = 0
    if not_selected_t is not None:
        parts.append(not_selected_t * (-MASK_BIG))
        used = not_selected_t.shape[0]
    if used < X_BLOCKS:
        parts.append(jnp.zeros((X_BLOCKS - used, n), F32))
    ridx = _iota((2 * N_PIECES, n), 0) & (N_PIECES - 1)
    parts.append(jnp.where(ridx == 0, pieces[0],
                           jnp.where(ridx == 1, pieces[1], jnp.where(ridx == 2, pieces[2], pieces[3]))))
    tail = HEAD_DIM - X_PAD
    parts.append(jnp.where(_iota((tail, n), 0) == 0, 1.0, 0.0))
    return jnp.concatenate(parts, axis=0)


def _augment_queries_t(qt_scaled, extras_t):
    return jnp.concatenate([qt_scaled, extras_t], axis=0).astype(BF16)


def _augment_keys(k_normed, extras):
    return jnp.concatenate([k_normed, extras], axis=1).astype(BF16)


def _scores_pass(q_aug_t, load_keys, n_rows, s_scr):
    m = None
    for c0 in range(0, n_rows, KEY_CHUNK):
        n = min(KEY_CHUNK, n_rows - c0)
        st = _dot(load_keys(c0, n), q_aug_t)
        s_scr[c0:c0 + n, :] = st
        mc = jnp.max(st, axis=0, keepdims=True)
        m = mc if m is None else jnp.maximum(m, mc)
    return m


def _probs_pass(s_scr, n_rows, m, load_vt):
    lanes = s_scr.shape[1]
    l = jnp.zeros((1, lanes), F32)
    acc = jnp.zeros((HEAD_DIM, lanes), F32)
    for c0 in range(0, n_rows, LANES):
        p = jnp.exp2(s_scr[c0:c0 + LANES, :] - m)
        l = l + jnp.sum(p, axis=0, keepdims=True)
        acc = acc + _dot(load_vt(c0 // LANES), p.astype(BF16))
    return l, acc


def _softmax_pv_t(st, vt_chunks):
    m = jnp.max(st, axis=0, keepdims=True)
    p = jnp.exp2(st - m)
    l = jnp.sum(p, axis=0, keepdims=True)
    pb = p.astype(BF16)
    acc = None
    for j, vt in enumerate(vt_chunks):
        d = _dot(vt, pb[j * LANES:(j + 1) * LANES, :])
        acc = d if acc is None else acc + d
    return m, l, acc


def _tile_lanes(x, reps):
    return jnp.concatenate([x] * reps, axis=1)


def _ffn_kernel(x_ref, g_ref, w1_ref, w3_ref, w2_ref, o_ref):
    x = x_ref[...]
    hb = _rms(x, g_ref[...]).astype(BF16)
    d_ff = w1_ref.shape[1]
    acc = jnp.zeros(x.shape, F32)
    for c in range(d_ff // FFN_FC):
        sl = slice(c * FFN_FC, (c + 1) * FFN_FC)
        a = _dot(hb, w1_ref[:, sl])
        b = _dot(hb, w3_ref[:, sl])
        u = (a * jax.nn.sigmoid(a) * b).astype(BF16)
        acc = acc + _dot(u, w2_ref[sl, :])
    o_ref[...] = x + 0.5 * acc


def _const_spec(shape):
    nd = len(shape)
    return pl.BlockSpec(shape, lambda *_: (0,) * nd, pipeline_mode=pl.Buffered(1))


def _params(n_axes):
    return pltpu.CompilerParams(dimension_semantics=("arbitrary",) * n_axes,
                                vmem_limit_bytes=VMEM_LIMIT_BYTES)


def _ffn(x, g, w1, w3, w2):
    n, d = x.shape
    d_ff = w1.shape[1]
    assert n % FFN_TM == 0 and d_ff % FFN_FC == 0
    return pl.pallas_call(
        _ffn_kernel,
        grid=(n // FFN_TM,),
        in_specs=[
            pl.BlockSpec((FFN_TM, d), lambda i: (i, 0)),
            _const_spec((1, d)),
            _const_spec((d, d_ff)),
            _const_spec((d, d_ff)),
            _const_spec((d_ff, d)),
        ],
        out_specs=pl.BlockSpec((FFN_TM, d), lambda i: (i, 0)),
        out_shape=jax.ShapeDtypeStruct((n, d), F32),
        compiler_params=_params(1),
        name="ffn",
    )(x, g.reshape(1, d), w1, w3, w2)


_ROW_KV, _ROW_KM, _ROW_GATE_N, _ROW_GATE_M, _ROW_COLS = 0, 512, 1024, 2048, 3072
_T_QN, _T_QM, _T_V, _T_VM, _T_GN, _T_ROWS = 0, 512, 1024, 1280, 1792, 1824
_GN_PAD = 16


def _inproj_kernel(x_ref, g_ref, w_ref, wt_ref, kv_ref, km_ref, gate_n_ref, gate_m_ref,
                   qnt_ref, qmt_ref, vt_ref, vmt_ref, gnt_ref):
    hb = _rms(x_ref[0], g_ref[...]).astype(BF16)
    n_chunks = hb.shape[0] // LANES

    def proj(off, width):
        return _dot(hb, w_ref[:, off:off + width])

    p = proj(_ROW_KV, 512)
    for a in range(4):
        for g in range(NSA_KV_HEADS):
            c0 = (a * NSA_KV_HEADS + g) * HEAD_DIM
            kv_ref[0, a, g] = p[:, c0:c0 + HEAD_DIM]
    p = proj(_ROW_KM, 512)
    for h in range(MOBA_HEADS):
        km_ref[0, h] = p[:, h * HEAD_DIM:(h + 1) * HEAD_DIM]
    gate_n_ref[0] = proj(_ROW_GATE_N, 1024)
    gate_m_ref[0] = proj(_ROW_GATE_M, 1024)

    def proj_t(off, height):
        return _dot_nt(wt_ref[off:off + height, :], hb)

    pt = proj_t(_T_QN, 512)
    for h in range(NSA_HEADS):
        qnt_ref[0, h] = pt[h * HEAD_DIM:(h + 1) * HEAD_DIM, :]
    pt = proj_t(_T_QM, 512)
    for h in range(MOBA_HEADS):
        qmt_ref[0, h] = pt[h * HEAD_DIM:(h + 1) * HEAD_DIM, :]
    pt = proj_t(_T_V, 256)
    for j in range(n_chunks):
        for a in range(2):
            for g in range(NSA_KV_HEADS):
                r0 = (a * NSA_KV_HEADS + g) * HEAD_DIM
                vt_ref[0, a, g, j] = pt[r0:r0 + HEAD_DIM, j * LANES:(j + 1) * LANES]
    pt = proj_t(_T_VM, 512)
    for j in range(n_chunks):
        for h in range(MOBA_HEADS):
            vmt_ref[0, h, j] = pt[h * HEAD_DIM:(h + 1) * HEAD_DIM, j * LANES:(j + 1) * LANES]
    pt = proj_t(_T_GN, NSA_KV_HEADS * _GN_PAD)
    for g in range(NSA_KV_HEADS):
        gnt_ref[0, g] = pt[g * _GN_PAD:(g + 1) * _GN_PAD, :]


def _inproj(x, g, w, wt):
    b, s, d = x.shape
    tm = FFN_TM
    assert s % tm == 0 and tm % LANES == 0
    nc, ncb = s // LANES, tm // LANES
    wide = pl.BlockSpec((1, tm, 1024), lambda bi, i: (bi, i, 0))
    qt_spec = pl.BlockSpec((1, NSA_HEADS, HEAD_DIM, tm), lambda bi, i: (bi, 0, 0, i))
    return pl.pallas_call(
        _inproj_kernel,
        grid=(b, s // tm),
        in_specs=[
            pl.BlockSpec((1, tm, d), lambda bi, i: (bi, i, 0)),
            _const_spec((1, d)),
            _const_spec((d, _ROW_COLS)),
            _const_spec((_T_ROWS, d)),
        ],
        out_specs=[
            pl.BlockSpec((1, 4, NSA_KV_HEADS, tm, HEAD_DIM), lambda bi, i: (bi, 0, 0, i, 0)),
            pl.BlockSpec((1, MOBA_HEADS, tm, HEAD_DIM), lambda bi, i: (bi, 0, i, 0)),
            wide, wide,
            qt_spec, qt_spec,
            pl.BlockSpec((1, 2, NSA_KV_HEADS, ncb, HEAD_DIM, LANES), lambda bi, i: (bi, 0, 0, i, 0, 0)),
            pl.BlockSpec((1, MOBA_HEADS, ncb, HEAD_DIM, LANES), lambda bi, i: (bi, 0, i, 0, 0)),
            pl.BlockSpec((1, NSA_KV_HEADS, _GN_PAD, tm), lambda bi, i: (bi, 0, 0, i)),
        ],
        out_shape=[
            jax.ShapeDtypeStruct((b, 4, NSA_KV_HEADS, s, HEAD_DIM), F32),
            jax.ShapeDtypeStruct((b, MOBA_HEADS, s, HEAD_DIM), F32),
            jax.ShapeDtypeStruct((b, s, 1024), F32),
            jax.ShapeDtypeStruct((b, s, 1024), F32),
            jax.ShapeDtypeStruct((b, NSA_HEADS, HEAD_DIM, s), F32),
            jax.ShapeDtypeStruct((b, MOBA_HEADS, HEAD_DIM, s), F32),
            jax.ShapeDtypeStruct((b, 2, NSA_KV_HEADS, nc, HEAD_DIM, LANES), F32),
            jax.ShapeDtypeStruct((b, MOBA_HEADS, nc, HEAD_DIM, LANES), F32),
            jax.ShapeDtypeStruct((b, NSA_KV_HEADS, _GN_PAD, s), F32),
        ],
        compiler_params=_params(2),
        name="inproj",
    )(x, g.reshape(1, d), w, wt)


def _compress_kernel(kc_ref, vc_ref, pos_ref, w1_ref, w2k_ref, w2vt_ref, gk_ref, kco_ref, vcto_ref):
    n_blocks = kc_ref.shape[3] // CMP_STRIDE
    for a, src in enumerate((kc_ref, vc_ref)):
        first = jnp.zeros((n_blocks, w1_ref.shape[2]), F32)
        second = jnp.zeros((n_blocks, w1_ref.shape[2]), F32)
        for p in range(CMP_STRIDE):
            tok = src[0, 0, 0, pl.ds(p, n_blocks, stride=CMP_STRIDE), :]
            q = p + CMP_STRIDE
            first = first + _dot((tok + pos_ref[a, p:p + 1, :]).astype(BF16),
                                 w1_ref[a, p * HEAD_DIM:(p + 1) * HEAD_DIM, :])
            second = second + _dot((tok + pos_ref[a, q:q + 1, :]).astype(BF16),
                                   w1_ref[a, q * HEAD_DIM:(q + 1) * HEAD_DIM, :])
        hid = first + pltpu.roll(second, n_blocks - 1, 0)
        act = jax.nn.gelu(hid).astype(BF16)
        if a == 0:
            y = _rms(_dot(act, w2k_ref[...]), gk_ref[...])
            kco_ref[0, 0] = jnp.where(_iota(y.shape, 0) < n_blocks - 1, y, 0.0)
        else:
            yt = _dot_nt(w2vt_ref[...], act)
            vcto_ref[0, 0] = jnp.where(_iota(yt.shape, 1) < n_blocks - 1, yt, 0.0)


def _compress(kv, pos, w1, w2k, w2vt, gk):
    b, _, g, s, _ = kv.shape
    n_blocks = s // CMP_STRIDE
    src = lambda a: pl.BlockSpec((1, 1, 1, s, HEAD_DIM), lambda bi, gi: (bi, a, gi, 0, 0))
    return pl.pallas_call(
        _compress_kernel,
        grid=(b, g),
        in_specs=[src(0), src(1), _const_spec(pos.shape), _const_spec(w1.shape),
                  _const_spec(w2k.shape), _const_spec(w2vt.shape), _const_spec((1, HEAD_DIM))],
        out_specs=[pl.BlockSpec((1, 1, n_blocks, HEAD_DIM), lambda bi, gi: (bi, gi, 0, 0)),
                   pl.BlockSpec((1, 1, HEAD_DIM, n_blocks), lambda bi, gi: (bi, gi, 0, 0))],
        out_shape=[jax.ShapeDtypeStruct((b, g, n_blocks, HEAD_DIM), F32),
                   jax.ShapeDtypeStruct((b, g, HEAD_DIM, n_blocks), F32)],
        compiler_params=_params(2),
        name="compress",
    )(kv, kv, pos, w1, w2k, w2vt, gk.reshape(1, HEAD_DIM))


def _nsa_kernel(coef_ref, qt_ref, kc_ref, vct_ref, ks_ref, kw_ref, vst_ref, vwt_ref,
                gnt_ref, gqk_ref, gqkt_ref, c2s_ref, o_ref,
                ksa_s, ksp_s, kwp_s, vst_s, vwt_s, cbias_s, wbias_s, s_scr, ml_s, acc_s):
    gi = pl.program_id(1)
    qi = pl.program_id(2)
    tq = NSA_TQ
    r_heads = NSA_GROUP
    lanes = r_heads * tq
    seq = ks_ref.shape[3]
    wk = tq + WINDOW
    n_wpad = WINDOW // LANES

    @pl.when(qi == 0)
    def _():
        kn = _rms(ks_ref[0, 0, 0], gqk_ref[2:3, :])
        plain = _key_extras(seq, 6, False)
        ksa_s[...] = _augment_keys(kn, _key_extras(seq, 6, True))
        ksp_s[...] = _augment_keys(kn, plain)
        kwp_s[0:WINDOW, :] = jnp.where(_iota((WINDOW, 2 * HEAD_DIM), 1) == HEAD_DIM + X_PAD,
                                       -MASK_BIG, 0.0).astype(BF16)
        kwp_s[WINDOW:, :] = _augment_keys(_rms(kw_ref[0, 0, 0], gqk_ref[3:4, :]), plain)
        vst_s[...] = vst_ref[0, 0, 0].astype(BF16)
        vwt_s[0:n_wpad] = jnp.zeros((n_wpad, HEAD_DIM, LANES), BF16)
        vwt_s[n_wpad:] = vwt_ref[0, 0, 0].astype(BF16)
        cbias_s[...] = jnp.where(_iota((tq, tq), 0) <= _iota((tq, tq), 1), 0.0, NEG_INF)
        wr, wi = _iota((wk, tq), 0), _iota((wk, tq), 1)
        wbias_s[...] = jnp.where((wr > wi) & (wr <= WINDOW + wi), 0.0, NEG_INF)

    start = qi * tq
    head = lambda r: gi * r_heads + r
    pieces = lambda r: [coef_ref[head(r), 1 + i] for i in range(N_PIECES)]
    qnt4 = _rms_t(qt_ref[0], gqkt_ref[:, 0:1])
    qnt = jnp.concatenate([qnt4[r] for r in range(r_heads)], axis=1)
    q_log2 = qnt * (SCALE * LOG2E)

    def lanes_of(x, r):
        return x[:, r * tq:(r + 1) * tq]

    q_plain = _augment_queries_t(
        q_log2, jnp.concatenate([_query_extras_t(None, pieces(r), tq) for r in range(r_heads)], axis=1))
    own = pl.multiple_of(start, tq)
    st = _dot(kwp_s[pl.ds(own, wk), :], q_plain) + _tile_lanes(wbias_s[...], r_heads)
    _, l_w, acc_w = _softmax_pv_t(st, [vwt_s[qi + j] for j in range(wk // LANES)])
    o_win_t = acc_w * (1.0 / l_w)
    st = _dot(ksp_s[pl.ds(own, tq), :], q_plain) + _tile_lanes(cbias_s[...], r_heads)
    m2, l2, acc2 = _softmax_pv_t(st, [vst_s[qi]])

    kc = kc_ref[0, 0]
    vct = vct_ref[0, 0].astype(BF16)
    n_c = kc.shape[0]
    sc_t = _dot_precise(kc, qnt * SCALE)
    t_lane = start + _iota((n_c, tq), 1)
    c_start = _iota((n_c, tq), 0) * CMP_STRIDE
    c_valid = (c_start + (CMP_BLOCK - 1)) <= t_lane
    c_dist = t_lane.astype(F32) - (c_start.astype(F32) + (CMP_BLOCK - 1) / 2)
    o_cmp_t = []
    p_sum = jnp.zeros((n_c, tq), F32)
    for r in range(r_heads):
        sm = jnp.where(c_valid, lanes_of(sc_t, r) - coef_ref[head(r), 0] * c_dist, NEG_INF)
        m = jnp.max(sm, axis=0, keepdims=True)
        p = jnp.where(c_valid, jnp.exp(sm - m), 0.0)
        l = jnp.sum(p, axis=0, keepdims=True)
        p = p * jnp.where(l > 0.0, 1.0 / l, 0.0)
        p_sum = p_sum + p
        o_cmp_t.append(_dot(vct, p.astype(BF16)))

    n_slc = c2s_ref.shape[0]
    ph, pl_ = _split(p_sum)
    c2s = c2s_ref[...].astype(BF16)
    imp_t = _dot(c2s, ph) + _dot(c2s, pl_)
    blk = _iota((n_slc, tq), 0)
    cur = lax.shift_right_logical(start + _iota((n_slc, tq), 1), 6)
    forced = (blk == 0) | (blk == cur) | (blk == cur - 1)
    sel_score = jnp.where(forced, FORCE, jnp.where(blk <= cur, imp_t, NEG_INF))
    sel = _topk_mask_t(sel_score, min(SLC_TOPK, n_slc))
    first_own = lax.shift_right_logical(start, 6)
    not_sel_t = jnp.where(sel & (blk < first_own), 0.0, 1.0)
    q_sel = _augment_queries_t(
        q_log2, jnp.concatenate([_query_extras_t(not_sel_t, pieces(r), tq) for r in range(r_heads)], axis=1))

    ml_s[0:1, :] = jnp.full((1, lanes), NEG_INF, F32)
    ml_s[1:2, :] = jnp.zeros((1, lanes), F32)
    acc_s[...] = jnp.zeros((HEAD_DIM, lanes), F32)
    per_bucket = NSA_KEY_BUCKET // tq
    for bk in range(1, seq // NSA_KEY_BUCKET + 1):
        @pl.when((qi + per_bucket - 1) // per_bucket == bk)
        def _(bk=bk):
            n_keys = bk * NSA_KEY_BUCKET
            m = _scores_pass(q_sel, lambda c0, n: ksa_s[c0:c0 + n, :], n_keys, s_scr)
            l, acc = _probs_pass(s_scr, n_keys, m, lambda j: vst_s[j])
            ml_s[0:1, :] = m
            ml_s[1:2, :] = l
            acc_s[...] = acc

    m1, l1 = ml_s[0:1, :], ml_s[1:2, :]
    m = jnp.maximum(m1, m2)
    w1, w2 = jnp.exp2(m1 - m), jnp.exp2(m2 - m)
    o_slc_t = (acc_s[...] * w1 + acc2 * w2) * (1.0 / (l1 * w1 + l2 * w2))

    gate = jax.nn.sigmoid(gnt_ref[0, 0])
    outs = []
    for r in range(r_heads):
        outs.append(gate[3 * r:3 * r + 1, :] * o_cmp_t[r]
                    + gate[3 * r + 1:3 * r + 2, :] * lanes_of(o_slc_t, r)
                    + gate[3 * r + 2:3 * r + 3, :] * lanes_of(o_win_t, r))
    o_ref[0] = jnp.concatenate(outs, axis=0).T.astype(o_ref.dtype)


def _nsa(coef, qnt, kc, vct, kv, vt, gnt, gqk, c2s):
    b, _, _, s = qnt.shape
    g = NSA_KV_HEADS
    tq = NSA_TQ
    n_c = kc.shape[2]
    nc = s // LANES
    lanes = NSA_GROUP * tq
    assert s % NSA_KEY_BUCKET == 0 and NSA_KEY_BUCKET % tq == 0 and WINDOW % LANES == 0
    assert tq == LANES and s // SLC_BLOCK <= X_BLOCKS
    kspec = lambda a: pl.BlockSpec((1, 1, 1, s, HEAD_DIM), lambda bi, gi, qi: (bi, a, gi, 0, 0))
    vspec = lambda a: pl.BlockSpec((1, 1, 1, nc, HEAD_DIM, LANES), lambda bi, gi, qi: (bi, a, gi, 0, 0, 0))
    whole = lambda a: pl.BlockSpec(a.shape, lambda bi, gi, qi: (0,) * a.ndim)
    gqk_t = gqk.T
    return pl.pallas_call(
        _nsa_kernel,
        grid=(b, g, s // tq),
        in_specs=[
            pl.BlockSpec(memory_space=pltpu.SMEM),
            pl.BlockSpec((1, NSA_GROUP, HEAD_DIM, tq), lambda bi, gi, qi: (bi, gi, 0, qi)),
            pl.BlockSpec((1, 1, n_c, HEAD_DIM), lambda bi, gi, qi: (bi, gi, 0, 0)),
            pl.BlockSpec((1, 1, HEAD_DIM, n_c), lambda bi, gi, qi: (bi, gi, 0, 0)),
            kspec(2), kspec(3), vspec(0), vspec(1),
            pl.BlockSpec((1, 1, _GN_PAD, tq), lambda bi, gi, qi: (bi, gi, 0, qi)),
            whole(gqk), whole(gqk_t), whole(c2s),
        ],
        out_specs=pl.BlockSpec((1, tq, NSA_GROUP * HEAD_DIM), lambda bi, gi, qi: (bi, qi, gi)),
        out_shape=jax.ShapeDtypeStruct((b, s, NSA_HEADS * HEAD_DIM), BF16),
        scratch_shapes=[
            pltpu.VMEM((s, 2 * HEAD_DIM), BF16),
            pltpu.VMEM((s, 2 * HEAD_DIM), BF16),
            pltpu.VMEM((s + WINDOW, 2 * HEAD_DIM), BF16),
            pltpu.VMEM((nc, HEAD_DIM, LANES), BF16),
            pltpu.VMEM((nc + WINDOW // LANES, HEAD_DIM, LANES), BF16),
            pltpu.VMEM((tq, tq), F32),
            pltpu.VMEM((tq + WINDOW, tq), F32),
            pltpu.VMEM((s, lanes), F32),
            pltpu.VMEM((8, lanes), F32),
            pltpu.VMEM((HEAD_DIM, lanes), F32),
        ],
        compiler_params=_params(3),
        name="nsa",
    )(coef, qnt, kc, vct, kv, kv, vt, vt, gnt, gqk, gqk_t, c2s)


def _moba_kernel(coef_ref, qt_ref, k_ref, vt_ref, gqk_ref, gqkt_ref, o_ref,
                 ka_s, kp_s, vt_s, kmean_s, cbias_s, s_scr):
    hp = pl.program_id(1)
    qi = pl.program_id(2)
    tq = MOBA_TQ
    blk = MOBA_BLOCK
    seq = k_ref.shape[2]
    n_blk = seq // blk
    heads = qt_ref.shape[1]

    @pl.when(qi == 0)
    def _():
        with_id = _key_extras(seq, 8, True)
        plain = _key_extras(seq, 8, False)
        for hh in range(heads):
            kn = _rms(k_ref[0, hh], gqk_ref[1:2, :])
            ka_s[hh] = _augment_keys(kn, with_id)
            kp_s[hh] = _augment_keys(kn, plain)
            vt_s[hh] = vt_ref[0, hh].astype(BF16)
            for n in range(n_blk):
                kmean_s[hh, n:n + 1, :] = jnp.mean(kn[n * blk:(n + 1) * blk], axis=0, keepdims=True)
        cbias_s[...] = jnp.where(_iota((blk, tq), 0) <= _iota((blk, tq), 1), 0.0, NEG_INF)

    def tile(n, hh):
        pieces = [coef_ref[hp * heads + hh, i] for i in range(N_PIECES)]
        qnt = _rms_t(qt_ref[0, hh], gqkt_ref[:, 0:1])
        not_sel_t = None
        if n > 0:
            gate_t = _dot_precise(kmean_s[hh], qnt)
            past = _iota((n_blk, tq), 0) < n
            top = _topk_mask_t(jnp.where(past, gate_t, NEG_INF), min(MOBA_TOPK, n_blk))
            not_sel_t = jnp.where(top & past, 0.0, 1.0)
        q_aug = _augment_queries_t(qnt * (SCALE * LOG2E), _query_extras_t(not_sel_t, pieces, tq))
        scr = s_scr.at[hh]
        n_past = n * blk
        st = _dot(kp_s[hh, n_past:n_past + blk, :], q_aug) + cbias_s[...]
        scr[n_past:n_past + blk, :] = st
        m = jnp.max(st, axis=0, keepdims=True)
        if n > 0:
            m = jnp.maximum(m, _scores_pass(q_aug, lambda c0, w: ka_s[hh, c0:c0 + w, :], n_past, scr))
        l, acc = _probs_pass(scr, n_past + blk, m, lambda j: vt_s[hh, j])
        return acc * (1.0 / l)

    for n in range(n_blk):
        @pl.when(qi == n)
        def _(n=n):
            o_t = jnp.concatenate([tile(n, hh) for hh in range(heads)], axis=0)
            o_ref[0] = o_t.T.astype(o_ref.dtype)


def _moba(coef, qt, k, vt, gqk):
    b, h, _, s = qt.shape
    tq = MOBA_TQ
    hpp = 2
    nc = s // LANES
    assert s % MOBA_BLOCK == 0 and h % hpp == 0 and s // MOBA_BLOCK <= X_BLOCKS
    whole = lambda a: pl.BlockSpec(a.shape, lambda bi, hi, qi: (0,) * a.ndim)
    gqk_t = gqk.T
    return pl.pallas_call(
        _moba_kernel,
        grid=(b, h // hpp, s // tq),
        in_specs=[
            pl.BlockSpec(memory_space=pltpu.SMEM),
            pl.BlockSpec((1, hpp, HEAD_DIM, tq), lambda bi, hi, qi: (bi, hi, 0, qi)),
            pl.BlockSpec((1, hpp, s, HEAD_DIM), lambda bi, hi, qi: (bi, hi, 0, 0)),
            pl.BlockSpec((1, hpp, nc, HEAD_DIM, LANES), lambda bi, hi, qi: (bi, hi, 0, 0, 0)),
            whole(gqk), whole(gqk_t),
        ],
        out_specs=pl.BlockSpec((1, tq, hpp * HEAD_DIM), lambda bi, hi, qi: (bi, qi, hi)),
        out_shape=jax.ShapeDtypeStruct((b, s, h * HEAD_DIM), BF16),
        scratch_shapes=[
            pltpu.VMEM((hpp, s, 2 * HEAD_DIM), BF16),
            pltpu.VMEM((hpp, s, 2 * HEAD_DIM), BF16),
            pltpu.VMEM((hpp, nc, HEAD_DIM, LANES), BF16),
            pltpu.VMEM((hpp, s // MOBA_BLOCK, HEAD_DIM), F32),
            pltpu.VMEM((MOBA_BLOCK, tq), F32),
            pltpu.VMEM((hpp, s, tq), F32),
        ],
        compiler_params=_params(3),
        name="moba",
    )(coef, qt, k, vt, gqk, gqk_t)


def _outproj_kernel(x_ref, on_ref, om_ref, gate_n_ref, gate_m_ref, wun_ref, wum_ref, wo_ref, o_ref):
    y = (jax.nn.sigmoid(gate_n_ref[...]) * _dot(on_ref[...], wun_ref[...])
         + jax.nn.sigmoid(gate_m_ref[...]) * _dot(om_ref[...], wum_ref[...]))
    o_ref[...] = x_ref[...] + _dot(y.astype(BF16), wo_ref[...])


def _outproj(x, o_n, o_m, gate_n, gate_m, wun, wum, wo):
    n, d = x.shape
    tm = FFN_TM
    row = lambda w: pl.BlockSpec((tm, w), lambda i: (i, 0))
    return pl.pallas_call(
        _outproj_kernel,
        grid=(n // tm,),
        in_specs=[row(d), row(o_n.shape[1]), row(o_m.shape[1]), row(d), row(d),
                  _const_spec(wun.shape), _const_spec(wum.shape), _const_spec(wo.shape)],
        out_specs=row(d),
        out_shape=jax.ShapeDtypeStruct((n, d), F32),
        compiler_params=_params(1),
        name="outproj",
    )(x, o_n, o_m, gate_n, gate_m, wun, wum, wo)


def _pack_w_in(w):
    d = w.shape[0]
    col = lambda a, b: w[:, a:b]
    rows = jnp.concatenate(
        [col(512, 640), col(640, 768), col(768, 896), col(1024, 1152),
         col(1816, 2328), col(2840, 3864), col(3864, 4888)], axis=1)
    gn = col(1280, 1304).reshape(d, NSA_KV_HEADS, NSA_GROUP * 3)
    gn = jnp.pad(gn, ((0, 0), (0, 0), (0, _GN_PAD - NSA_GROUP * 3))).reshape(d, NSA_KV_HEADS * _GN_PAD)
    transposed = jnp.concatenate(
        [col(0, 512), col(1304, 1816), col(896, 1024), col(1152, 1280), col(2328, 2840), gn], axis=1).T
    return rows.astype(BF16), transposed.astype(BF16)


def _alibi_coefficients():
    slopes = (2.0 ** (-8.0 * np.arange(1, TOTAL_HEADS + 1) / TOTAL_HEADS)).astype(np.float32)
    rem = slopes.astype(np.float64) * LOG2E
    cols = [slopes]
    for _ in range(N_PIECES):
        piece = rem.astype(BF16).astype(np.float64)
        cols.append(piece.astype(np.float32))
        rem = rem - piece
    table = np.stack(cols, axis=1)
    return jnp.asarray(table[0::2]), jnp.asarray(table[1::2, 1:])


def _cmp_to_slc_t(n_blocks, n_slc):
    ci = np.arange(n_blocks)[None, :] * CMP_STRIDE
    sj = np.arange(n_slc)[:, None] * SLC_BLOCK
    overlap = np.clip(np.minimum(ci + CMP_BLOCK, sj + SLC_BLOCK) - np.maximum(ci, sj), 0, None)
    m = (overlap / CMP_BLOCK).astype(np.float32)
    m[:, n_blocks - 1] = 0.0
    return jnp.asarray(m)


def kernel(x, norm_g, ffn_w1, ffn_w3, ffn_w2, w_in, g_qk_nsa, g_qk_moba, cmp_pos, cmp_w1, cmp_w2,
           w_up_nsa, w_up_moba, w_out):
    b, s, d = x.shape
    depth = norm_g.shape[0]
    coef_nsa, coef_moba = _alibi_coefficients()
    c2s_t = _cmp_to_slc_t(s // CMP_STRIDE, s // SLC_BLOCK)
    xf = x.reshape(b * s, d)
    for l in range(depth):
        xf = _ffn(xf, norm_g[l, 0], ffn_w1[l, 0].astype(BF16), ffn_w3[l, 0].astype(BF16),
                  ffn_w2[l, 0].astype(BF16))
        w_rows, w_t = _pack_w_in(w_in[l])
        kv, km, gate_n, gate_m, qnt, qmt, vt, vmt, gnt = _inproj(
            xf.reshape(b, s, d), norm_g[l, 1], w_rows, w_t)
        kc, vct = _compress(kv, cmp_pos[l], cmp_w1[l].astype(BF16), cmp_w2[l, 0].astype(BF16),
                            cmp_w2[l, 1].T.astype(BF16), g_qk_nsa[l, 1])
        o_n = _nsa(coef_nsa, qnt, kc, vct, kv, vt, gnt, g_qk_nsa[l], c2s_t)
        o_m = _moba(coef_moba, qmt, km, vmt, g_qk_moba[l])
        xf = _outproj(xf, o_n.reshape(b * s, -1), o_m.reshape(b * s, -1),
                      gate_n.reshape(b * s, -1), gate_m.reshape(b * s, -1),
                      w_up_nsa[l].astype(BF16), w_up_moba[l].astype(BF16), w_out[l].astype(BF16))
        xf = _ffn(xf, norm_g[l, 2], ffn_w1[l, 1].astype(BF16), ffn_w3[l, 1].astype(BF16),
                  ffn_w2[l, 1].astype(BF16))
    return xf.reshape(b, s, d)
```

```python
import numpy as np
import jax
import jax.numpy as jnp
from jax import lax
from jax.experimental import pallas as pl
from jax.experimental.pallas import tpu as pltpu

F32 = jnp.float32
BF16 = jnp.bfloat16

HEAD_DIM = 64
NSA_HEADS = 8
NSA_KV_HEADS = 2
NSA_GROUP = NSA_HEADS // NSA_KV_HEADS
CMP_BLOCK = 32
CMP_STRIDE = 16
SLC_BLOCK = 64
SLC_TOPK = 8
WINDOW = 512
MOBA_HEADS = 8
MOBA_BLOCK = 256
MOBA_TOPK = 3
TOTAL_HEADS = NSA_HEADS + MOBA_HEADS
NEG_INF = -1e30
FORCE = 1e9
RMS_EPS = 1e-6
SCALE = HEAD_DIM ** -0.5
LOG2E = 1.4426950408889634

VMEM_LIMIT_BYTES = 56 * 1024 * 1024
LANES = 128

FFN_TM = 512
FFN_FC = 256
NSA_TQ = 128
NSA_KEY_BUCKET = 256
MOBA_TQ = MOBA_BLOCK
KEY_CHUNK = 256

X_BLOCKS = 32
N_PIECES = 4
X_HI = X_BLOCKS
X_LO = X_HI + N_PIECES
X_PAD = X_LO + N_PIECES
MASK_BIG = 2.0 ** 20


def _rms(x, g):
    ms = jnp.mean(x * x, axis=-1, keepdims=True)
    return x * lax.rsqrt(ms + RMS_EPS) * g


def _rms_t(xt, g_col):
    ms = jnp.mean(xt * xt, axis=-2, keepdims=True)
    return xt * lax.rsqrt(ms + RMS_EPS) * g_col


def _dot(a, b):
    return jnp.dot(a, b, preferred_element_type=F32)


def _dot_nt(a, b):
    return lax.dot_general(a, b, (((1,), (1,)), ((), ())), preferred_element_type=F32)


def _split(a):
    hi = a.astype(BF16)
    lo = (a - hi.astype(F32)).astype(BF16)
    return hi, lo


def _dot_precise(a, b):
    ah, al = _split(a)
    bh, bl = _split(b)
    return _dot(ah, bh) + (_dot(ah, bl) + _dot(al, bh))


def _iota(shape, dim):
    return lax.broadcasted_iota(jnp.int32, shape, dim)


def _topk_mask_t(score, k):
    rows = score.shape[0]
    ridx = _iota(score.shape, 0)
    rank = jnp.zeros(score.shape, jnp.int32)
    for j in range(rows):
        sj = score[j:j + 1, :]
        beats = (sj > score) | ((sj == score) & (ridx > j))
        rank = rank + beats.astype(jnp.int32)
    return rank < k


def _key_extras(n_rows, block_shift, with_block_id):
    pos = _iota((n_rows, HEAD_DIM), 0)
    lane = _iota((n_rows, HEAD_DIM), 1)
    hi = (lax.shift_right_logical(pos, 6) * 64).astype(F32)
    lo = (pos & 63).astype(F32)
    x = jnp.where((lane >= X_HI) & (lane < X_LO), hi,
                  jnp.where((lane >= X_LO) & (lane < X_PAD), lo, 0.0))
    if with_block_id:
        x = jnp.where(lax.shift_right_logical(pos, block_shift) == lane, 1.0, x)
    return x


def _query_extras_t(not_selected_t, pieces, n):
    parts = []
    used = 0
    if not_selected_t is not None:
        parts.append(not_selected_t * (-MASK_BIG))
        used = not_selected_t.shape[0]
    if used < X_BLOCKS:
        parts.append(jnp.zeros((X_BLOCKS - used, n), F32))
    ridx = _iota((2 * N_PIECES, n), 0) & (N_PIECES - 1)
    parts.append(jnp.where(ridx == 0, pieces[0],
                           jnp.where(ridx == 1, pieces[1], jnp.where(ridx == 2, pieces[2], pieces[3]))))
    tail = HEAD_DIM - X_PAD
    parts.append(jnp.where(_iota((tail, n), 0) == 0, 1.0, 0.0))
    return jnp.concatenate(parts, axis=0)


def _augment_queries_t(qt_scaled, extras_t):
    return jnp.concatenate([qt_scaled, extras_t], axis=0).astype(BF16)


def _augment_keys(k_normed, extras):
    return jnp.concatenate([k_normed, extras], axis=1).astype(BF16)


def _scores_pass(q_aug_t, load_keys, n_rows, s_scr):
    m = None
    for c0 in range(0, n_rows, KEY_CHUNK):
        n = min(KEY_CHUNK, n_rows - c0)
        st = _dot(load_keys(c0, n), q_aug_t)
        s_scr[c0:c0 + n, :] = st
        mc = jnp.max(st, axis=0, keepdims=True)
        m = mc if m is None else jnp.maximum(m, mc)
    return m


def _probs_pass(s_scr, n_rows, m, load_vt):
    lanes = s_scr.shape[1]
    l = jnp.zeros((1, lanes), F32)
    acc = jnp.zeros((HEAD_DIM, lanes), F32)
    for c0 in range(0, n_rows, LANES):
        p = jnp.exp2(s_scr[c0:c0 + LANES, :] - m)
        l = l + jnp.sum(p, axis=0, keepdims=True)
        acc = acc + _dot(load_vt(c0 // LANES), p.astype(BF16))
    return l, acc


def _softmax_pv_t(st, vt_chunks):
    m = jnp.max(st, axis=0, keepdims=True)
    p = jnp.exp2(st - m)
    l = jnp.sum(p, axis=0, keepdims=True)
    pb = p.astype(BF16)
    acc = None
    for j, vt in enumerate(vt_chunks):
        d = _dot(vt, pb[j * LANES:(j + 1) * LANES, :])
        acc = d if acc is None else acc + d
    return m, l, acc


def _tile_lanes(x, reps):
    return jnp.concatenate([x] * reps, axis=1)


def _ffn_kernel(x_ref, g_ref, w1_ref, w3_ref, w2_ref, o_ref):
    x = x_ref[...]
    hb = _rms(x, g_ref[...]).astype(BF16)
    d_ff = w1_ref.shape[1]
    acc = jnp.zeros(x.shape, F32)
    for c in range(d_ff // FFN_FC):
        sl = slice(c * FFN_FC, (c + 1) * FFN_FC)
        a = _dot(hb, w1_ref[:, sl])
        b = _dot(hb, w3_ref[:, sl])
        u = (a * jax.nn.sigmoid(a) * b).astype(BF16)
        acc = acc + _dot(u, w2_ref[sl, :])
    o_ref[...] = x + 0.5 * acc


def _const_spec(shape):
    nd = len(shape)
    return pl.BlockSpec(shape, lambda *_: (0,) * nd, pipeline_mode=pl.Buffered(1))


def _params(n_axes):
    return pltpu.CompilerParams(dimension_semantics=("arbitrary",) * n_axes,
                                vmem_limit_bytes=VMEM_LIMIT_BYTES)


def _ffn(x, g, w1, w3, w2):
    n, d = x.shape
    d_ff = w1.shape[1]
    assert n % FFN_TM == 0 and d_ff % FFN_FC == 0
    return pl.pallas_call(
        _ffn_kernel,
        grid=(n // FFN_TM,),
        in_specs=[
            pl.BlockSpec((FFN_TM, d), lambda i: (i, 0)),
            _const_spec((1, d)),
            _const_spec((d, d_ff)),
            _const_spec((d, d_ff)),
            _const_spec((d_ff, d)),
        ],
        out_specs=pl.BlockSpec((FFN_TM, d), lambda i: (i, 0)),
        out_shape=jax.ShapeDtypeStruct((n, d), F32),
        compiler_params=_params(1),
        name="ffn",
    )(x, g.reshape(1, d), w1, w3, w2)


_ROW_KV, _ROW_KM, _ROW_GATE_N, _ROW_GATE_M, _ROW_COLS = 0, 512, 1024, 2048, 3072
_T_QN, _T_QM, _T_V, _T_VM, _T_GN, _T_ROWS = 0, 512, 1024, 1280, 1792, 1824
_GN_PAD = 16


def _inproj_kernel(x_ref, g_ref, w_ref, wt_ref, kv_ref, km_ref, gate_n_ref, gate_m_ref,
                   qnt_ref, qmt_ref, vt_ref, vmt_ref, gnt_ref):
    hb = _rms(x_ref[0], g_ref[...]).astype(BF16)
    n_chunks = hb.shape[0] // LANES

    def proj(off, width):
        return _dot(hb, w_ref[:, off:off + width])

    p = proj(_ROW_KV, 512)
    for a in range(4):
        for g in range(NSA_KV_HEADS):
            c0 = (a * NSA_KV_HEADS + g) * HEAD_DIM
            kv_ref[0, a, g] = p[:, c0:c0 + HEAD_DIM]
    p = proj(_ROW_KM, 512)
    for h in range(MOBA_HEADS):
        km_ref[0, h] = p[:, h * HEAD_DIM:(h + 1) * HEAD_DIM]
    gate_n_ref[0] = proj(_ROW_GATE_N, 1024)
    gate_m_ref[0] = proj(_ROW_GATE_M, 1024)

    def proj_t(off, height):
        return _dot_nt(wt_ref[off:off + height, :], hb)

    pt = proj_t(_T_QN, 512)
    for h in range(NSA_HEADS):
        qnt_ref[0, h] = pt[h * HEAD_DIM:(h + 1) * HEAD_DIM, :]
    pt = proj_t(_T_QM, 512)
    for h in range(MOBA_HEADS):
        qmt_ref[0, h] = pt[h * HEAD_DIM:(h + 1) * HEAD_DIM, :]
    pt = proj_t(_T_V, 256)
    for j in range(n_chunks):
        for a in range(2):
            for g in range(NSA_KV_HEADS):
                r0 = (a * NSA_KV_HEADS + g) * HEAD_DIM
                vt_ref[0, a, g, j] = pt[r0:r0 + HEAD_DIM, j * LANES:(j + 1) * LANES]
    pt = proj_t(_T_VM, 512)
    for j in range(n_chunks):
        for h in range(MOBA_HEADS):
            vmt_ref[0, h, j] = pt[h * HEAD_DIM:(h + 1) * HEAD_DIM, j * LANES:(j + 1) * LANES]
    pt = proj_t(_T_GN, NSA_KV_HEADS * _GN_PAD)
    for g in range(NSA_KV_HEADS):
        gnt_ref[0, g] = pt[g * _GN_PAD:(g + 1) * _GN_PAD, :]


def _inproj(x, g, w, wt):
    b, s, d = x.shape
    tm = FFN_TM
    assert s % tm == 0 and tm % LANES == 0
    nc, ncb = s // LANES, tm // LANES
    wide = pl.BlockSpec((1, tm, 1024), lambda bi, i: (bi, i, 0))
    qt_spec = pl.BlockSpec((1, NSA_HEADS, HEAD_DIM, tm), lambda bi, i: (bi, 0, 0, i))
    return pl.pallas_call(
        _inproj_kernel,
        grid=(b, s // tm),
        in_specs=[
            pl.BlockSpec((1, tm, d), lambda bi, i: (bi, i, 0)),
            _const_spec((1, d)),
            _const_spec((d, _ROW_COLS)),
            _const_spec((_T_ROWS, d)),
        ],
        out_specs=[
            pl.BlockSpec((1, 4, NSA_KV_HEADS, tm, HEAD_DIM), lambda bi, i: (bi, 0, 0, i, 0)),
            pl.BlockSpec((1, MOBA_HEADS, tm, HEAD_DIM), lambda bi, i: (bi, 0, i, 0)),
            wide, wide,
            qt_spec, qt_spec,
            pl.BlockSpec((1, 2, NSA_KV_HEADS, ncb, HEAD_DIM, LANES), lambda bi, i: (bi, 0, 0, i, 0, 0)),
            pl.BlockSpec((1, MOBA_HEADS, ncb, HEAD_DIM, LANES), lambda bi, i: (bi, 0, i, 0, 0)),
            pl.BlockSpec((1, NSA_KV_HEADS, _GN_PAD, tm), lambda bi, i: (bi, 0, 0, i)),
        ],
        out_shape=[
            jax.ShapeDtypeStruct((b, 4, NSA_KV_HEADS, s, HEAD_DIM), F32),
            jax.ShapeDtypeStruct((b, MOBA_HEADS, s, HEAD_DIM), F32),
            jax.ShapeDtypeStruct((b, s, 1024), F32),
            jax.ShapeDtypeStruct((b, s, 1024), F32),
            jax.ShapeDtypeStruct((b, NSA_HEADS, HEAD_DIM, s), F32),
            jax.ShapeDtypeStruct((b, MOBA_HEADS, HEAD_DIM, s), F32),
            jax.ShapeDtypeStruct((b, 2, NSA_KV_HEADS, nc, HEAD_DIM, LANES), F32),
            jax.ShapeDtypeStruct((b, MOBA_HEADS, nc, HEAD_DIM, LANES), F32),
            jax.ShapeDtypeStruct((b, NSA_KV_HEADS, _GN_PAD, s), F32),
        ],
        compiler_params=_params(2),
        name="inproj",
    )(x, g.reshape(1, d), w, wt)


def _compress_kernel(kc_ref, vc_ref, pos_ref, w1_ref, w2k_ref, w2vt_ref, gk_ref, kco_ref, vcto_ref):
    n_blocks = kc_ref.shape[3] // CMP_STRIDE
    for a, src in enumerate((kc_ref, vc_ref)):
        first = jnp.zeros((n_blocks, w1_ref.shape[2]), F32)
        second = jnp.zeros((n_blocks, w1_ref.shape[2]), F32)
        for p in range(CMP_STRIDE):
            tok = src[0, 0, 0, pl.ds(p, n_blocks, stride=CMP_STRIDE), :]
            q = p + CMP_STRIDE
            first = first + _dot((tok + pos_ref[a, p:p + 1, :]).astype(BF16),
                                 w1_ref[a, p * HEAD_DIM:(p + 1) * HEAD_DIM, :])
            second = second + _dot((tok + pos_ref[a, q:q + 1, :]).astype(BF16),
                                   w1_ref[a, q * HEAD_DIM:(q + 1) * HEAD_DIM, :])
        hid = first + pltpu.roll(second, n_blocks - 1, 0)
        act = jax.nn.gelu(hid).astype(BF16)
        if a == 0:
            y = _rms(_dot(act, w2k_ref[...]), gk_ref[...])
            kco_ref[0, 0] = jnp.where(_iota(y.shape, 0) < n_blocks - 1, y, 0.0)
        else:
            yt = _dot_nt(w2vt_ref[...], act)
            vcto_ref[0, 0] = jnp.where(_iota(yt.shape, 1) < n_blocks - 1, yt, 0.0)


def _compress(kv, pos, w1, w2k, w2vt, gk):
    b, _, g, s, _ = kv.shape
    n_blocks = s // CMP_STRIDE
    src = lambda a: pl.BlockSpec((1, 1, 1, s, HEAD_DIM), lambda bi, gi: (bi, a, gi, 0, 0))
    return pl.pallas_call(
        _compress_kernel,
        grid=(b, g),
        in_specs=[src(0), src(1), _const_spec(pos.shape), _const_spec(w1.shape),
                  _const_spec(w2k.shape), _const_spec(w2vt.shape), _const_spec((1, HEAD_DIM))],
        out_specs=[pl.BlockSpec((1, 1, n_blocks, HEAD_DIM), lambda bi, gi: (bi, gi, 0, 0)),
                   pl.BlockSpec((1, 1, HEAD_DIM, n_blocks), lambda bi, gi: (bi, gi, 0, 0))],
        out_shape=[jax.ShapeDtypeStruct((b, g, n_blocks, HEAD_DIM), F32),
                   jax.ShapeDtypeStruct((b, g, HEAD_DIM, n_blocks), F32)],
        compiler_params=_params(2),
        name="compress",
    )(kv, kv, pos, w1, w2k, w2vt, gk.reshape(1, HEAD_DIM))


def _nsa_kernel(coef_ref, qt_ref, kc_ref, vct_ref, ks_ref, kw_ref, vst_ref, vwt_ref,
                gnt_ref, gqk_ref, gqkt_ref, c2s_ref, o_ref,
                ksa_s, ksp_s, kwp_s, vst_s, vwt_s, cbias_s, wbias_s, s_scr, ml_s, acc_s):
    qi = pl.program_id(1)
    tq = NSA_TQ
    r_heads = NSA_GROUP
    groups = NSA_KV_HEADS
    lanes = r_heads * tq
    seq = ks_ref.shape[3]
    wk = tq + WINDOW
    n_wpad = WINDOW // LANES

    @pl.when(qi == 0)
    def _():
        plain = _key_extras(seq, 6, False)
        with_id = _key_extras(seq, 6, True)
        pad_rows = jnp.where(_iota((WINDOW, 2 * HEAD_DIM), 1) == HEAD_DIM + X_PAD, -MASK_BIG, 0.0).astype(BF16)
        for g in range(groups):
            kn = _rms(ks_ref[0, 0, g], gqk_ref[2:3, :])
            ksa_s[g] = _augment_keys(kn, with_id)
            ksp_s[g] = _augment_keys(kn, plain)
            kwp_s[g, 0:WINDOW, :] = pad_rows
            kwp_s[g, WINDOW:, :] = _augment_keys(_rms(kw_ref[0, 0, g], gqk_ref[3:4, :]), plain)
            vst_s[g] = vst_ref[0, 0, g].astype(BF16)
            vwt_s[g, 0:n_wpad] = jnp.zeros((n_wpad, HEAD_DIM, LANES), BF16)
            vwt_s[g, n_wpad:] = vwt_ref[0, 0, g].astype(BF16)
        cbias_s[...] = jnp.where(_iota((tq, tq), 0) <= _iota((tq, tq), 1), 0.0, NEG_INF)
        wr, wi = _iota((wk, tq), 0), _iota((wk, tq), 1)
        wbias_s[...] = jnp.where((wr > wi) & (wr <= WINDOW + wi), 0.0, NEG_INF)

    start = qi * tq
    own = pl.multiple_of(start, tq)

    def lanes_of(x, r):
        return x[:, r * tq:(r + 1) * tq]

    def before_buckets(g):
        head = lambda r: g * r_heads + r
        pieces = lambda r: [coef_ref[head(r), 1 + i] for i in range(N_PIECES)]
        qnt4 = _rms_t(qt_ref[0, g * r_heads:(g + 1) * r_heads], gqkt_ref[:, 0:1])
        qnt = jnp.concatenate([qnt4[r] for r in range(r_heads)], axis=1)
        q_log2 = qnt * (SCALE * LOG2E)

        q_plain = _augment_queries_t(
            q_log2, jnp.concatenate([_query_extras_t(None, pieces(r), tq) for r in range(r_heads)], axis=1))
        st = _dot(kwp_s[g, pl.ds(own, wk), :], q_plain) + _tile_lanes(wbias_s[...], r_heads)
        _, l_w, acc_w = _softmax_pv_t(st, [vwt_s[g, qi + j] for j in range(wk // LANES)])
        o_win_t = acc_w * (1.0 / l_w)
        st = _dot(ksp_s[g, pl.ds(own, tq), :], q_plain) + _tile_lanes(cbias_s[...], r_heads)
        own_part = _softmax_pv_t(st, [vst_s[g, qi]])

        kc = kc_ref[0, g]
        vct = vct_ref[0, g].astype(BF16)
        n_c = kc.shape[0]
        sc_t = _dot_precise(kc, qnt * SCALE)
        t_lane = start + _iota((n_c, tq), 1)
        c_start = _iota((n_c, tq), 0) * CMP_STRIDE
        c_valid = (c_start + (CMP_BLOCK - 1)) <= t_lane
        c_dist = t_lane.astype(F32) - (c_start.astype(F32) + (CMP_BLOCK - 1) / 2)
        o_cmp_t = []
        p_sum = jnp.zeros((n_c, tq), F32)
        for r in range(r_heads):
            sm = jnp.where(c_valid, lanes_of(sc_t, r) - coef_ref[head(r), 0] * c_dist, NEG_INF)
            m = jnp.max(sm, axis=0, keepdims=True)
            p = jnp.where(c_valid, jnp.exp(sm - m), 0.0)
            l = jnp.sum(p, axis=0, keepdims=True)
            p = p * jnp.where(l > 0.0, 1.0 / l, 0.0)
            p_sum = p_sum + p
            o_cmp_t.append(_dot(vct, p.astype(BF16)))

        n_slc = c2s_ref.shape[0]
        ph, pl_ = _split(p_sum)
        c2s = c2s_ref[...].astype(BF16)
        imp_t = _dot(c2s, ph) + _dot(c2s, pl_)
        blk = _iota((n_slc, tq), 0)
        cur = lax.shift_right_logical(start + _iota((n_slc, tq), 1), 6)
        forced = (blk == 0) | (blk == cur) | (blk == cur - 1)
        sel_score = jnp.where(forced, FORCE, jnp.where(blk <= cur, imp_t, NEG_INF))
        sel = _topk_mask_t(sel_score, min(SLC_TOPK, n_slc))
        first_own = lax.shift_right_logical(start, 6)
        not_sel_t = jnp.where(sel & (blk < first_own), 0.0, 1.0)
        q_sel = _augment_queries_t(
            q_log2,
            jnp.concatenate([_query_extras_t(not_sel_t, pieces(r), tq) for r in range(r_heads)], axis=1))
        return q_sel, own_part, o_win_t, o_cmp_t

    state = [before_buckets(g) for g in range(groups)]

    for g in range(groups):
        ml_s[g, 0:1, :] = jnp.full((1, lanes), NEG_INF, F32)
        ml_s[g, 1:2, :] = jnp.zeros((1, lanes), F32)
        acc_s[g] = jnp.zeros((HEAD_DIM, lanes), F32)
    per_bucket = NSA_KEY_BUCKET // tq
    for bk in range(1, seq // NSA_KEY_BUCKET + 1):
        @pl.when((qi + per_bucket - 1) // per_bucket == bk)
        def _(bk=bk):
            n_keys = bk * NSA_KEY_BUCKET
            ms = [_scores_pass(state[g][0], lambda c0, n, g=g: ksa_s[g, c0:c0 + n, :], n_keys, s_scr.at[g])
                  for g in range(groups)]
            for g in range(groups):
                l, acc = _probs_pass(s_scr.at[g], n_keys, ms[g], lambda j, g=g: vst_s[g, j])
                ml_s[g, 0:1, :] = ms[g]
                ml_s[g, 1:2, :] = l
                acc_s[g] = acc

    outs = []
    for g in range(groups):
        _, (m2, l2, acc2), o_win_t, o_cmp_t = state[g]
        m1, l1 = ml_s[g, 0:1, :], ml_s[g, 1:2, :]
        m = jnp.maximum(m1, m2)
        w1, w2 = jnp.exp2(m1 - m), jnp.exp2(m2 - m)
        o_slc_t = (acc_s[g] * w1 + acc2 * w2) * (1.0 / (l1 * w1 + l2 * w2))
        gate = jax.nn.sigmoid(gnt_ref[0, g])
        for r in range(r_heads):
            outs.append(gate[3 * r:3 * r + 1, :] * o_cmp_t[r]
                        + gate[3 * r + 1:3 * r + 2, :] * lanes_of(o_slc_t, r)
                        + gate[3 * r + 2:3 * r + 3, :] * lanes_of(o_win_t, r))
    o_ref[0] = jnp.concatenate(outs, axis=0).T.astype(o_ref.dtype)


def _nsa(coef, qnt, kc, vct, kv, vt, gnt, gqk, c2s):
    b, _, _, s = qnt.shape
    g = NSA_KV_HEADS
    tq = NSA_TQ
    n_c = kc.shape[2]
    nc = s // LANES
    lanes = NSA_GROUP * tq
    assert s % NSA_KEY_BUCKET == 0 and NSA_KEY_BUCKET % tq == 0 and WINDOW % LANES == 0
    assert tq == LANES and s // SLC_BLOCK <= X_BLOCKS
    kspec = lambda a: pl.BlockSpec((1, 1, g, s, HEAD_DIM), lambda bi, qi: (bi, a, 0, 0, 0))
    vspec = lambda a: pl.BlockSpec((1, 1, g, nc, HEAD_DIM, LANES), lambda bi, qi: (bi, a, 0, 0, 0, 0))
    whole = lambda a: pl.BlockSpec(a.shape, lambda bi, qi: (0,) * a.ndim)
    gqk_t = gqk.T
    return pl.pallas_call(
        _nsa_kernel,
        grid=(b, s // tq),
        in_specs=[
            pl.BlockSpec(memory_space=pltpu.SMEM),
            pl.BlockSpec((1, NSA_HEADS, HEAD_DIM, tq), lambda bi, qi: (bi, 0, 0, qi)),
            pl.BlockSpec((1, g, n_c, HEAD_DIM), lambda bi, qi: (bi, 0, 0, 0)),
            pl.BlockSpec((1, g, HEAD_DIM, n_c), lambda bi, qi: (bi, 0, 0, 0)),
            kspec(2), kspec(3), vspec(0), vspec(1),
            pl.BlockSpec((1, g, _GN_PAD, tq), lambda bi, qi: (bi, 0, 0, qi)),
            whole(gqk), whole(gqk_t), whole(c2s),
        ],
        out_specs=pl.BlockSpec((1, tq, NSA_HEADS * HEAD_DIM), lambda bi, qi: (bi, qi, 0)),
        out_shape=jax.ShapeDtypeStruct((b, s, NSA_HEADS * HEAD_DIM), BF16),
        scratch_shapes=[
            pltpu.VMEM((g, s, 2 * HEAD_DIM), BF16),
            pltpu.VMEM((g, s, 2 * HEAD_DIM), BF16),
            pltpu.VMEM((g, s + WINDOW, 2 * HEAD_DIM), BF16),
            pltpu.VMEM((g, nc, HEAD_DIM, LANES), BF16),
            pltpu.VMEM((g, nc + WINDOW // LANES, HEAD_DIM, LANES), BF16),
            pltpu.VMEM((tq, tq), F32),
            pltpu.VMEM((tq + WINDOW, tq), F32),
            pltpu.VMEM((g, s, lanes), F32),
            pltpu.VMEM((g, 8, lanes), F32),
            pltpu.VMEM((g, HEAD_DIM, lanes), F32),
        ],
        compiler_params=_params(2),
        name="nsa",
    )(coef, qnt, kc, vct, kv, kv, vt, vt, gnt, gqk, gqk_t, c2s)


def _moba_kernel(coef_ref, qt_ref, k_ref, vt_ref, gqk_ref, gqkt_ref, o_ref,
                 ka_s, kp_s, vt_s, kmean_s, cbias_s, s_scr):
    hp = pl.program_id(1)
    qi = pl.program_id(2)
    tq = MOBA_TQ
    blk = MOBA_BLOCK
    seq = k_ref.shape[2]
    n_blk = seq // blk
    heads = qt_ref.shape[1]

    @pl.when(qi == 0)
    def _():
        with_id = _key_extras(seq, 8, True)
        plain = _key_extras(seq, 8, False)
        for hh in range(heads):
            kn = _rms(k_ref[0, hh], gqk_ref[1:2, :])
            ka_s[hh] = _augment_keys(kn, with_id)
            kp_s[hh] = _augment_keys(kn, plain)
            vt_s[hh] = vt_ref[0, hh].astype(BF16)
            for n in range(n_blk):
                kmean_s[hh, n:n + 1, :] = jnp.mean(kn[n * blk:(n + 1) * blk], axis=0, keepdims=True)
        cbias_s[...] = jnp.where(_iota((blk, tq), 0) <= _iota((blk, tq), 1), 0.0, NEG_INF)

    def tile(n, hh):
        pieces = [coef_ref[hp * heads + hh, i] for i in range(N_PIECES)]
        qnt = _rms_t(qt_ref[0, hh], gqkt_ref[:, 0:1])
        not_sel_t = None
        if n > 0:
            gate_t = _dot_precise(kmean_s[hh], qnt)
            past = _iota((n_blk, tq), 0) < n
            top = _topk_mask_t(jnp.where(past, gate_t, NEG_INF), min(MOBA_TOPK, n_blk))
            not_sel_t = jnp.where(top & past, 0.0, 1.0)
        q_aug = _augment_queries_t(qnt * (SCALE * LOG2E), _query_extras_t(not_sel_t, pieces, tq))
        scr = s_scr.at[hh]
        n_past = n * blk
        st = _dot(kp_s[hh, n_past:n_past + blk, :], q_aug) + cbias_s[...]
        scr[n_past:n_past + blk, :] = st
        m = jnp.max(st, axis=0, keepdims=True)
        if n > 0:
            m = jnp.maximum(m, _scores_pass(q_aug, lambda c0, w: ka_s[hh, c0:c0 + w, :], n_past, scr))
        l, acc = _probs_pass(scr, n_past + blk, m, lambda j: vt_s[hh, j])
        return acc * (1.0 / l)

    for n in range(n_blk):
        @pl.when(qi == n)
        def _(n=n):
            o_t = jnp.concatenate([tile(n, hh) for hh in range(heads)], axis=0)
            o_ref[0] = o_t.T.astype(o_ref.dtype)


def _moba(coef, qt, k, vt, gqk):
    b, h, _, s = qt.shape
    tq = MOBA_TQ
    hpp = 4
    nc = s // LANES
    assert s % MOBA_BLOCK == 0 and h % hpp == 0 and s // MOBA_BLOCK <= X_BLOCKS
    whole = lambda a: pl.BlockSpec(a.shape, lambda bi, hi, qi: (0,) * a.ndim)
    gqk_t = gqk.T
    return pl.pallas_call(
        _moba_kernel,
        grid=(b, h // hpp, s // tq),
        in_specs=[
            pl.BlockSpec(memory_space=pltpu.SMEM),
            pl.BlockSpec((1, hpp, HEAD_DIM, tq), lambda bi, hi, qi: (bi, hi, 0, qi)),
            pl.BlockSpec((1, hpp, s, HEAD_DIM), lambda bi, hi, qi: (bi, hi, 0, 0)),
            pl.BlockSpec((1, hpp, nc, HEAD_DIM, LANES), lambda bi, hi, qi: (bi, hi, 0, 0, 0)),
            whole(gqk), whole(gqk_t),
        ],
        out_specs=pl.BlockSpec((1, tq, hpp * HEAD_DIM), lambda bi, hi, qi: (bi, qi, hi)),
        out_shape=jax.ShapeDtypeStruct((b, s, h * HEAD_DIM), BF16),
        scratch_shapes=[
            pltpu.VMEM((hpp, s, 2 * HEAD_DIM), BF16),
            pltpu.VMEM((hpp, s, 2 * HEAD_DIM), BF16),
            pltpu.VMEM((hpp, nc, HEAD_DIM, LANES), BF16),
            pltpu.VMEM((hpp, s // MOBA_BLOCK, HEAD_DIM), F32),
            pltpu.VMEM((MOBA_BLOCK, tq), F32),
            pltpu.VMEM((hpp, s, tq), F32),
        ],
        compiler_params=_params(3),
        name="moba",
    )(coef, qt, k, vt, gqk, gqk_t)


def _outproj_kernel(x_ref, on_ref, om_ref, gate_n_ref, gate_m_ref, wun_ref, wum_ref, wo_ref, o_ref):
    y = (jax.nn.sigmoid(gate_n_ref[...]) * _dot(on_ref[...], wun_ref[...])
         + jax.nn.sigmoid(gate_m_ref[...]) * _dot(om_ref[...], wum_ref[...]))
    o_ref[...] = x_ref[...] + _dot(y.astype(BF16), wo_ref[...])


def _outproj(x, o_n, o_m, gate_n, gate_m, wun, wum, wo):
    n, d = x.shape
    tm = FFN_TM
    row = lambda w: pl.BlockSpec((tm, w), lambda i: (i, 0))
    return pl.pallas_call(
        _outproj_kernel,
        grid=(n // tm,),
        in_specs=[row(d), row(o_n.shape[1]), row(o_m.shape[1]), row(d), row(d),
                  _const_spec(wun.shape), _const_spec(wum.shape), _const_spec(wo.shape)],
        out_specs=row(d),
        out_shape=jax.ShapeDtypeStruct((n, d), F32),
        compiler_params=_params(1),
        name="outproj",
    )(x, o_n, o_m, gate_n, gate_m, wun, wum, wo)


def _pack_w_in(w):
    d = w.shape[0]
    col = lambda a, b: w[:, a:b]
    rows = jnp.concatenate(
        [col(512, 640), col(640, 768), col(768, 896), col(1024, 1152),
         col(1816, 2328), col(2840, 3864), col(3864, 4888)], axis=1)
    gn = col(1280, 1304).reshape(d, NSA_KV_HEADS, NSA_GROUP * 3)
    gn = jnp.pad(gn, ((0, 0), (0, 0), (0, _GN_PAD - NSA_GROUP * 3))).reshape(d, NSA_KV_HEADS * _GN_PAD)
    transposed = jnp.concatenate(
        [col(0, 512), col(1304, 1816), col(896, 1024), col(1152, 1280), col(2328, 2840), gn], axis=1).T
    return rows.astype(BF16), transposed.astype(BF16)


def _alibi_coefficients():
    slopes = (2.0 ** (-8.0 * np.arange(1, TOTAL_HEADS + 1) / TOTAL_HEADS)).astype(np.float32)
    rem = slopes.astype(np.float64) * LOG2E
    cols = [slopes]
    for _ in range(N_PIECES):
        piece = rem.astype(BF16).astype(np.float64)
        cols.append(piece.astype(np.float32))
        rem = rem - piece
    table = np.stack(cols, axis=1)
    return jnp.asarray(table[0::2]), jnp.asarray(table[1::2, 1:])


def _cmp_to_slc_t(n_blocks, n_slc):
    ci = np.arange(n_blocks)[None, :] * CMP_STRIDE
    sj = np.arange(n_slc)[:, None] * SLC_BLOCK
    overlap = np.clip(np.minimum(ci + CMP_BLOCK, sj + SLC_BLOCK) - np.maximum(ci, sj), 0, None)
    m = (overlap / CMP_BLOCK).astype(np.float32)
    m[:, n_blocks - 1] = 0.0
    return jnp.asarray(m)


def kernel(x, norm_g, ffn_w1, ffn_w3, ffn_w2, w_in, g_qk_nsa, g_qk_moba, cmp_pos, cmp_w1, cmp_w2,
           w_up_nsa, w_up_moba, w_out):
    b, s, d = x.shape
    depth = norm_g.shape[0]
    coef_nsa, coef_moba = _alibi_coefficients()
    c2s_t = _cmp_to_slc_t(s // CMP_STRIDE, s // SLC_BLOCK)
    xf = x.reshape(b * s, d)
    for l in range(depth):
        xf = _ffn(xf, norm_g[l, 0], ffn_w1[l, 0].astype(BF16), ffn_w3[l, 0].astype(BF16),
                  ffn_w2[l, 0].astype(BF16))
        w_rows, w_t = _pack_w_in(w_in[l])
        kv, km, gate_n, gate_m, qnt, qmt, vt, vmt, gnt = _inproj(
            xf.reshape(b, s, d), norm_g[l, 1], w_rows, w_t)
        kc, vct = _compress(kv, cmp_pos[l], cmp_w1[l].astype(BF16), cmp_w2[l, 0].astype(BF16),
                            cmp_w2[l, 1].T.astype(BF16), g_qk_nsa[l, 1])
        o_n = _nsa(coef_nsa, qnt, kc, vct, kv, vt, gnt, g_qk_nsa[l], c2s_t)
        o_m = _moba(coef_moba, qmt, km, vmt, g_qk_moba[l])
        xf = _outproj(xf, o_n.reshape(b * s, -1), o_m.reshape(b * s, -1),
                      gate_n.reshape(b * s, -1), gate_m.reshape(b * s, -1),
                      w_up_nsa[l].astype(BF16), w_up_moba[l].astype(BF16), w_out[l].astype(BF16))
        xf = _ffn(xf, norm_g[l, 2], ffn_w1[l, 1].astype(BF16), ffn_w3[l, 1].astype(BF16),
                  ffn_w2[l, 1].astype(BF16))
    return xf.reshape(b, s, d)
```

```python
import numpy as np
import jax
import jax.numpy as jnp
from jax import lax
from jax.experimental import pallas as pl
from jax.experimental.pallas import tpu as pltpu

F32 = jnp.float32
BF16 = jnp.bfloat16

HEAD_DIM = 64
NSA_HEADS = 8
NSA_KV_HEADS = 2
NSA_GROUP = NSA_HEADS // NSA_KV_HEADS
CMP_BLOCK = 32
CMP_STRIDE = 16
SLC_BLOCK = 64
SLC_TOPK = 8
WINDOW = 512
MOBA_HEADS = 8
MOBA_BLOCK = 256
MOBA_TOPK = 3
TOTAL_HEADS = NSA_HEADS + MOBA_HEADS
NEG_INF = -1e30
FORCE = 1e9
RMS_EPS = 1e-6
SCALE = HEAD_DIM ** -0.5
LOG2E = 1.4426950408889634

VMEM_LIMIT_BYTES = 56 * 1024 * 1024
LANES = 128

FFN_TM = 512
FFN_FC = 256
NSA_TQ = 128
NSA_KEY_BUCKET = 256
MOBA_TQ = MOBA_BLOCK
KEY_CHUNK = 256

X_BLOCKS = 32
N_PIECES = 4
X_HI = X_BLOCKS
X_LO = X_HI + N_PIECES
X_PAD = X_LO + N_PIECES
MASK_BIG = 2.0 ** 20


def _rms(x, g):
    ms = jnp.mean(x * x, axis=-1, keepdims=True)
    return x * lax.rsqrt(ms + RMS_EPS) * g


def _rms_t(xt, g_col):
    ms = jnp.mean(xt * xt, axis=-2, keepdims=True)
    return xt * lax.rsqrt(ms + RMS_EPS) * g_col


def _dot(a, b):
    return jnp.dot(a, b, preferred_element_type=F32)


def _dot_nt(a, b):
    return lax.dot_general(a, b, (((1,), (1,)), ((), ())), preferred_element_type=F32)


def _split(a):
    hi = a.astype(BF16)
    lo = (a - hi.astype(F32)).astype(BF16)
    return hi, lo


def _dot_precise(a, b):
    ah, al = _split(a)
    bh, bl = _split(b)
    return _dot(ah, bh) + (_dot(ah, bl) + _dot(al, bh))


def _iota(shape, dim):
    return lax.broadcasted_iota(jnp.int32, shape, dim)


def _topk_mask_t(score, k):
    rows = score.shape[0]
    ridx = _iota(score.shape, 0)
    rank = jnp.zeros(score.shape, jnp.int32)
    for j in range(rows):
        sj = score[j:j + 1, :]
        beats = (sj > score) | ((sj == score) & (ridx > j))
        rank = rank + beats.astype(jnp.int32)
    return rank < k


def _key_extras(n_rows, block_shift, with_block_id):
    pos = _iota((n_rows, HEAD_DIM), 0)
    lane = _iota((n_rows, HEAD_DIM), 1)
    hi = (lax.shift_right_logical(pos, 6) * 64).astype(F32)
    lo = (pos & 63).astype(F32)
    x = jnp.where((lane >= X_HI) & (lane < X_LO), hi,
                  jnp.where((lane >= X_LO) & (lane < X_PAD), lo, 0.0))
    if with_block_id:
        x = jnp.where(lax.shift_right_logical(pos, block_shift) == lane, 1.0, x)
    return x


def _query_extras_t(not_selected_t, pieces, n):
    parts = []
    used = 0
    if not_selected_t is not None:
        parts.append(not_selected_t * (-MASK_BIG))
        used = not_selected_t.shape[0]
    if used < X_BLOCKS:
        parts.append(jnp.zeros((X_BLOCKS - used, n), F32))
    ridx = _iota((2 * N_PIECES, n), 0) & (N_PIECES - 1)
    parts.append(jnp.where(ridx == 0, pieces[0],
                           jnp.where(ridx == 1, pieces[1], jnp.where(ridx == 2, pieces[2], pieces[3]))))
    tail = HEAD_DIM - X_PAD
    parts.append(jnp.where(_iota((tail, n), 0) == 0, 1.0, 0.0))
    return jnp.concatenate(parts, axis=0)


def _augment_queries_t(qt_scaled, extras_t):
    return jnp.concatenate([qt_scaled, extras_t], axis=0).astype(BF16)


def _augment_keys(k_normed, extras):
    return jnp.concatenate([k_normed, extras], axis=1).astype(BF16)


def _scores_pass(q_aug_t, load_keys, n_rows, s_scr):
    m = None
    for c0 in range(0, n_rows, KEY_CHUNK):
        n = min(KEY_CHUNK, n_rows - c0)
        st = _dot(load_keys(c0, n), q_aug_t)
        s_scr[c0:c0 + n, :] = st
        mc = jnp.max(st, axis=0, keepdims=True)
        m = mc if m is None else jnp.maximum(m, mc)
    return m


def _probs_pass(s_scr, n_rows, m, load_vt):
    lanes = s_scr.shape[1]
    l = jnp.zeros((1, lanes), F32)
    acc = jnp.zeros((HEAD_DIM, lanes), F32)
    for c0 in range(0, n_rows, LANES):
        p = jnp.exp2(s_scr[c0:c0 + LANES, :] - m)
        l = l + jnp.sum(p, axis=0, keepdims=True)
        acc = acc + _dot(load_vt(c0 // LANES), p.astype(BF16))
    return l, acc


def _softmax_pv_t(st, vt_chunks):
    m = jnp.max(st, axis=0, keepdims=True)
    p = jnp.exp2(st - m)
    l = jnp.sum(p, axis=0, keepdims=True)
    pb = p.astype(BF16)
    acc = None
    for j, vt in enumerate(vt_chunks):
        d = _dot(vt, pb[j * LANES:(j + 1) * LANES, :])
        acc = d if acc is None else acc + d
    return m, l, acc


def _tile_lanes(x, reps):
    return jnp.concatenate([x] * reps, axis=1)


def _ffn_half_step(x, g_ref, w1_ref, w3_ref, w2_ref):
    hb = _rms(x, g_ref[...]).astype(BF16)
    d_ff = w1_ref.shape[1]
    acc = jnp.zeros(x.shape, F32)
    for c in range(d_ff // FFN_FC):
        sl = slice(c * FFN_FC, (c + 1) * FFN_FC)
        a = _dot(hb, w1_ref[:, sl])
        b = _dot(hb, w3_ref[:, sl])
        u = (a * jax.nn.sigmoid(a) * b).astype(BF16)
        acc = acc + _dot(u, w2_ref[sl, :])
    return x + 0.5 * acc


def _ffn_kernel(x_ref, g_ref, w1_ref, w3_ref, w2_ref, o_ref):
    o_ref[...] = _ffn_half_step(x_ref[...], g_ref, w1_ref, w3_ref, w2_ref)


def _const_spec(shape):
    nd = len(shape)
    return pl.BlockSpec(shape, lambda *_: (0,) * nd, pipeline_mode=pl.Buffered(1))


def _params(n_axes):
    return pltpu.CompilerParams(dimension_semantics=("arbitrary",) * n_axes,
                                vmem_limit_bytes=VMEM_LIMIT_BYTES)


def _ffn(x, g, w1, w3, w2):
    n, d = x.shape
    d_ff = w1.shape[1]
    assert n % FFN_TM == 0 and d_ff % FFN_FC == 0
    return pl.pallas_call(
        _ffn_kernel,
        grid=(n // FFN_TM,),
        in_specs=[
            pl.BlockSpec((FFN_TM, d), lambda i: (i, 0)),
            _const_spec((1, d)),
            _const_spec((d, d_ff)),
            _const_spec((d, d_ff)),
            _const_spec((d_ff, d)),
        ],
        out_specs=pl.BlockSpec((FFN_TM, d), lambda i: (i, 0)),
        out_shape=jax.ShapeDtypeStruct((n, d), F32),
        compiler_params=_params(1),
        name="ffn",
    )(x, g.reshape(1, d), w1, w3, w2)


_ROW_KV, _ROW_KM, _ROW_GATE_N, _ROW_GATE_M, _ROW_COLS = 0, 512, 1024, 2048, 3072
_T_QN, _T_QM, _T_V, _T_VM, _T_GN, _T_ROWS = 0, 512, 1024, 1280, 1792, 1824
_GN_PAD = 16


def _inproj_kernel(x_ref, g_ref, w_ref, wt_ref, kv_ref, km_ref, gate_n_ref, gate_m_ref,
                   qnt_ref, qmt_ref, vt_ref, vmt_ref, gnt_ref):
    hb = _rms(x_ref[0], g_ref[...]).astype(BF16)
    n_chunks = hb.shape[0] // LANES

    def proj(off, width):
        return _dot(hb, w_ref[:, off:off + width])

    p = proj(_ROW_KV, 512)
    for a in range(4):
        for g in range(NSA_KV_HEADS):
            c0 = (a * NSA_KV_HEADS + g) * HEAD_DIM
            kv_ref[0, a, g] = p[:, c0:c0 + HEAD_DIM]
    p = proj(_ROW_KM, 512)
    for h in range(MOBA_HEADS):
        km_ref[0, h] = p[:, h * HEAD_DIM:(h + 1) * HEAD_DIM]
    gate_n_ref[0] = proj(_ROW_GATE_N, 1024)
    gate_m_ref[0] = proj(_ROW_GATE_M, 1024)

    def proj_t(off, height):
        return _dot_nt(wt_ref[off:off + height, :], hb)

    pt = proj_t(_T_QN, 512)
    for h in range(NSA_HEADS):
        qnt_ref[0, h] = pt[h * HEAD_DIM:(h + 1) * HEAD_DIM, :]
    pt = proj_t(_T_QM, 512)
    for h in range(MOBA_HEADS):
        qmt_ref[0, h] = pt[h * HEAD_DIM:(h + 1) * HEAD_DIM, :]
    pt = proj_t(_T_V, 256)
    for j in range(n_chunks):
        for a in range(2):
            for g in range(NSA_KV_HEADS):
                r0 = (a * NSA_KV_HEADS + g) * HEAD_DIM
                vt_ref[0, a, g, j] = pt[r0:r0 + HEAD_DIM, j * LANES:(j + 1) * LANES]
    pt = proj_t(_T_VM, 512)
    for j in range(n_chunks):
        for h in range(MOBA_HEADS):
            vmt_ref[0, h, j] = pt[h * HEAD_DIM:(h + 1) * HEAD_DIM, j * LANES:(j + 1) * LANES]
    pt = proj_t(_T_GN, NSA_KV_HEADS * _GN_PAD)
    for g in range(NSA_KV_HEADS):
        gnt_ref[0, g] = pt[g * _GN_PAD:(g + 1) * _GN_PAD, :]


def _inproj(x, g, w, wt):
    b, s, d = x.shape
    tm = FFN_TM
    assert s % tm == 0 and tm % LANES == 0
    nc, ncb = s // LANES, tm // LANES
    wide = pl.BlockSpec((1, tm, 1024), lambda bi, i: (bi, i, 0))
    qt_spec = pl.BlockSpec((1, NSA_HEADS, HEAD_DIM, tm), lambda bi, i: (bi, 0, 0, i))
    return pl.pallas_call(
        _inproj_kernel,
        grid=(b, s // tm),
        in_specs=[
            pl.BlockSpec((1, tm, d), lambda bi, i: (bi, i, 0)),
            _const_spec((1, d)),
            _const_spec((d, _ROW_COLS)),
            _const_spec((_T_ROWS, d)),
        ],
        out_specs=[
            pl.BlockSpec((1, 4, NSA_KV_HEADS, tm, HEAD_DIM), lambda bi, i: (bi, 0, 0, i, 0)),
            pl.BlockSpec((1, MOBA_HEADS, tm, HEAD_DIM), lambda bi, i: (bi, 0, i, 0)),
            wide, wide,
            qt_spec, qt_spec,
            pl.BlockSpec((1, 2, NSA_KV_HEADS, ncb, HEAD_DIM, LANES), lambda bi, i: (bi, 0, 0, i, 0, 0)),
            pl.BlockSpec((1, MOBA_HEADS, ncb, HEAD_DIM, LANES), lambda bi, i: (bi, 0, i, 0, 0)),
            pl.BlockSpec((1, NSA_KV_HEADS, _GN_PAD, tm), lambda bi, i: (bi, 0, 0, i)),
        ],
        out_shape=[
            jax.ShapeDtypeStruct((b, 4, NSA_KV_HEADS, s, HEAD_DIM), F32),
            jax.ShapeDtypeStruct((b, MOBA_HEADS, s, HEAD_DIM), F32),
            jax.ShapeDtypeStruct((b, s, 1024), F32),
            jax.ShapeDtypeStruct((b, s, 1024), F32),
            jax.ShapeDtypeStruct((b, NSA_HEADS, HEAD_DIM, s), F32),
            jax.ShapeDtypeStruct((b, MOBA_HEADS, HEAD_DIM, s), F32),
            jax.ShapeDtypeStruct((b, 2, NSA_KV_HEADS, nc, HEAD_DIM, LANES), F32),
            jax.ShapeDtypeStruct((b, MOBA_HEADS, nc, HEAD_DIM, LANES), F32),
            jax.ShapeDtypeStruct((b, NSA_KV_HEADS, _GN_PAD, s), F32),
        ],
        compiler_params=_params(2),
        name="inproj",
    )(x, g.reshape(1, d), w, wt)


def _compress_kernel(kc_ref, vc_ref, pos_ref, w1_ref, w2k_ref, w2vt_ref, gk_ref, kco_ref, vcto_ref):
    n_blocks = kc_ref.shape[3] // CMP_STRIDE
    for a, src in enumerate((kc_ref, vc_ref)):
        first = jnp.zeros((n_blocks, w1_ref.shape[2]), F32)
        second = jnp.zeros((n_blocks, w1_ref.shape[2]), F32)
        for p in range(CMP_STRIDE):
            tok = src[0, 0, 0, pl.ds(p, n_blocks, stride=CMP_STRIDE), :]
            q = p + CMP_STRIDE
            first = first + _dot((tok + pos_ref[a, p:p + 1, :]).astype(BF16),
                                 w1_ref[a, p * HEAD_DIM:(p + 1) * HEAD_DIM, :])
            second = second + _dot((tok + pos_ref[a, q:q + 1, :]).astype(BF16),
                                   w1_ref[a, q * HEAD_DIM:(q + 1) * HEAD_DIM, :])
        hid = first + pltpu.roll(second, n_blocks - 1, 0)
        act = jax.nn.gelu(hid).astype(BF16)
        if a == 0:
            y = _rms(_dot(act, w2k_ref[...]), gk_ref[...])
            kco_ref[0, 0] = jnp.where(_iota(y.shape, 0) < n_blocks - 1, y, 0.0)
        else:
            yt = _dot_nt(w2vt_ref[...], act)
            vcto_ref[0, 0] = jnp.where(_iota(yt.shape, 1) < n_blocks - 1, yt, 0.0)


def _compress(kv, pos, w1, w2k, w2vt, gk):
    b, _, g, s, _ = kv.shape
    n_blocks = s // CMP_STRIDE
    src = lambda a: pl.BlockSpec((1, 1, 1, s, HEAD_DIM), lambda bi, gi: (bi, a, gi, 0, 0))
    return pl.pallas_call(
        _compress_kernel,
        grid=(b, g),
        in_specs=[src(0), src(1), _const_spec(pos.shape), _const_spec(w1.shape),
                  _const_spec(w2k.shape), _const_spec(w2vt.shape), _const_spec((1, HEAD_DIM))],
        out_specs=[pl.BlockSpec((1, 1, n_blocks, HEAD_DIM), lambda bi, gi: (bi, gi, 0, 0)),
                   pl.BlockSpec((1, 1, HEAD_DIM, n_blocks), lambda bi, gi: (bi, gi, 0, 0))],
        out_shape=[jax.ShapeDtypeStruct((b, g, n_blocks, HEAD_DIM), F32),
                   jax.ShapeDtypeStruct((b, g, HEAD_DIM, n_blocks), F32)],
        compiler_params=_params(2),
        name="compress",
    )(kv, kv, pos, w1, w2k, w2vt, gk.reshape(1, HEAD_DIM))


def _nsa_kernel(coef_ref, qt_ref, kc_ref, vct_ref, ks_ref, kw_ref, vst_ref, vwt_ref,
                gnt_ref, gqk_ref, gqkt_ref, c2s_ref, o_ref,
                ksa_s, ksp_s, kwp_s, vst_s, vwt_s, cbias_s, wbias_s, s_scr, ml_s, acc_s):
    qi = pl.program_id(1)
    tq = NSA_TQ
    r_heads = NSA_GROUP
    groups = NSA_KV_HEADS
    lanes = r_heads * tq
    seq = ks_ref.shape[3]
    wk = tq + WINDOW
    n_wpad = WINDOW // LANES

    @pl.when(qi == 0)
    def _():
        plain = _key_extras(seq, 6, False)
        with_id = _key_extras(seq, 6, True)
        pad_rows = jnp.where(_iota((WINDOW, 2 * HEAD_DIM), 1) == HEAD_DIM + X_PAD, -MASK_BIG, 0.0).astype(BF16)
        for g in range(groups):
            kn = _rms(ks_ref[0, 0, g], gqk_ref[2:3, :])
            ksa_s[g] = _augment_keys(kn, with_id)
            ksp_s[g] = _augment_keys(kn, plain)
            kwp_s[g, 0:WINDOW, :] = pad_rows
            kwp_s[g, WINDOW:, :] = _augment_keys(_rms(kw_ref[0, 0, g], gqk_ref[3:4, :]), plain)
            vst_s[g] = vst_ref[0, 0, g].astype(BF16)
            vwt_s[g, 0:n_wpad] = jnp.zeros((n_wpad, HEAD_DIM, LANES), BF16)
            vwt_s[g, n_wpad:] = vwt_ref[0, 0, g].astype(BF16)
        cbias_s[...] = jnp.where(_iota((tq, tq), 0) <= _iota((tq, tq), 1), 0.0, NEG_INF)
        wr, wi = _iota((wk, tq), 0), _iota((wk, tq), 1)
        wbias_s[...] = jnp.where((wr > wi) & (wr <= WINDOW + wi), 0.0, NEG_INF)

    start = qi * tq
    own = pl.multiple_of(start, tq)

    def lanes_of(x, r):
        return x[:, r * tq:(r + 1) * tq]

    def before_buckets(g):
        head = lambda r: g * r_heads + r
        pieces = lambda r: [coef_ref[head(r), 1 + i] for i in range(N_PIECES)]
        qnt4 = _rms_t(qt_ref[0, g * r_heads:(g + 1) * r_heads], gqkt_ref[:, 0:1])
        qnt = jnp.concatenate([qnt4[r] for r in range(r_heads)], axis=1)
        q_log2 = qnt * (SCALE * LOG2E)

        q_plain = _augment_queries_t(
            q_log2, jnp.concatenate([_query_extras_t(None, pieces(r), tq) for r in range(r_heads)], axis=1))
        st = _dot(kwp_s[g, pl.ds(own, wk), :], q_plain) + _tile_lanes(wbias_s[...], r_heads)
        _, l_w, acc_w = _softmax_pv_t(st, [vwt_s[g, qi + j] for j in range(wk // LANES)])
        o_win_t = acc_w * (1.0 / l_w)
        st = _dot(ksp_s[g, pl.ds(own, tq), :], q_plain) + _tile_lanes(cbias_s[...], r_heads)
        own_part = _softmax_pv_t(st, [vst_s[g, qi]])

        kc = kc_ref[0, g]
        vct = vct_ref[0, g].astype(BF16)
        n_c = kc.shape[0]
        sc_t = _dot_precise(kc, qnt * SCALE)
        t_lane = start + _iota((n_c, tq), 1)
        c_start = _iota((n_c, tq), 0) * CMP_STRIDE
        c_valid = (c_start + (CMP_BLOCK - 1)) <= t_lane
        c_dist = t_lane.astype(F32) - (c_start.astype(F32) + (CMP_BLOCK - 1) / 2)
        o_cmp_t = []
        p_sum = jnp.zeros((n_c, tq), F32)
        for r in range(r_heads):
            sm = jnp.where(c_valid, lanes_of(sc_t, r) - coef_ref[head(r), 0] * c_dist, NEG_INF)
            m = jnp.max(sm, axis=0, keepdims=True)
            p = jnp.where(c_valid, jnp.exp(sm - m), 0.0)
            l = jnp.sum(p, axis=0, keepdims=True)
            p = p * jnp.where(l > 0.0, 1.0 / l, 0.0)
            p_sum = p_sum + p
            o_cmp_t.append(_dot(vct, p.astype(BF16)))

        n_slc = c2s_ref.shape[0]
        ph, pl_ = _split(p_sum)
        c2s = c2s_ref[...].astype(BF16)
        imp_t = _dot(c2s, ph) + _dot(c2s, pl_)
        blk = _iota((n_slc, tq), 0)
        cur = lax.shift_right_logical(start + _iota((n_slc, tq), 1), 6)
        forced = (blk == 0) | (blk == cur) | (blk == cur - 1)
        sel_score = jnp.where(forced, FORCE, jnp.where(blk <= cur, imp_t, NEG_INF))
        sel = _topk_mask_t(sel_score, min(SLC_TOPK, n_slc))
        first_own = lax.shift_right_logical(start, 6)
        not_sel_t = jnp.where(sel & (blk < first_own), 0.0, 1.0)
        q_sel = _augment_queries_t(
            q_log2,
            jnp.concatenate([_query_extras_t(not_sel_t, pieces(r), tq) for r in range(r_heads)], axis=1))
        return q_sel, own_part, o_win_t, o_cmp_t

    state = [before_buckets(g) for g in range(groups)]

    for g in range(groups):
        ml_s[g, 0:1, :] = jnp.full((1, lanes), NEG_INF, F32)
        ml_s[g, 1:2, :] = jnp.zeros((1, lanes), F32)
        acc_s[g] = jnp.zeros((HEAD_DIM, lanes), F32)
    per_bucket = NSA_KEY_BUCKET // tq
    for bk in range(1, seq // NSA_KEY_BUCKET + 1):
        @pl.when((qi + per_bucket - 1) // per_bucket == bk)
        def _(bk=bk):
            n_keys = bk * NSA_KEY_BUCKET
            ms = [_scores_pass(state[g][0], lambda c0, n, g=g: ksa_s[g, c0:c0 + n, :], n_keys, s_scr.at[g])
                  for g in range(groups)]
            for g in range(groups):
                l, acc = _probs_pass(s_scr.at[g], n_keys, ms[g], lambda j, g=g: vst_s[g, j])
                ml_s[g, 0:1, :] = ms[g]
                ml_s[g, 1:2, :] = l
                acc_s[g] = acc

    outs = []
    for g in range(groups):
        _, (m2, l2, acc2), o_win_t, o_cmp_t = state[g]
        m1, l1 = ml_s[g, 0:1, :], ml_s[g, 1:2, :]
        m = jnp.maximum(m1, m2)
        w1, w2 = jnp.exp2(m1 - m), jnp.exp2(m2 - m)
        o_slc_t = (acc_s[g] * w1 + acc2 * w2) * (1.0 / (l1 * w1 + l2 * w2))
        gate = jax.nn.sigmoid(gnt_ref[0, g])
        for r in range(r_heads):
            outs.append(gate[3 * r:3 * r + 1, :] * o_cmp_t[r]
                        + gate[3 * r + 1:3 * r + 2, :] * lanes_of(o_slc_t, r)
                        + gate[3 * r + 2:3 * r + 3, :] * lanes_of(o_win_t, r))
    o_ref[0] = jnp.concatenate(outs, axis=0).T.astype(o_ref.dtype)


def _nsa(coef, qnt, kc, vct, kv, vt, gnt, gqk, c2s):
    b, _, _, s = qnt.shape
    g = NSA_KV_HEADS
    tq = NSA_TQ
    n_c = kc.shape[2]
    nc = s // LANES
    lanes = NSA_GROUP * tq
    assert s % NSA_KEY_BUCKET == 0 and NSA_KEY_BUCKET % tq == 0 and WINDOW % LANES == 0
    assert tq == LANES and s // SLC_BLOCK <= X_BLOCKS
    kspec = lambda a: pl.BlockSpec((1, 1, g, s, HEAD_DIM), lambda bi, qi: (bi, a, 0, 0, 0))
    vspec = lambda a: pl.BlockSpec((1, 1, g, nc, HEAD_DIM, LANES), lambda bi, qi: (bi, a, 0, 0, 0, 0))
    whole = lambda a: pl.BlockSpec(a.shape, lambda bi, qi: (0,) * a.ndim)
    gqk_t = gqk.T
    return pl.pallas_call(
        _nsa_kernel,
        grid=(b, s // tq),
        in_specs=[
            pl.BlockSpec(memory_space=pltpu.SMEM),
            pl.BlockSpec((1, NSA_HEADS, HEAD_DIM, tq), lambda bi, qi: (bi, 0, 0, qi)),
            pl.BlockSpec((1, g, n_c, HEAD_DIM), lambda bi, qi: (bi, 0, 0, 0)),
            pl.BlockSpec((1, g, HEAD_DIM, n_c), lambda bi, qi: (bi, 0, 0, 0)),
            kspec(2), kspec(3), vspec(0), vspec(1),
            pl.BlockSpec((1, g, _GN_PAD, tq), lambda bi, qi: (bi, 0, 0, qi)),
            whole(gqk), whole(gqk_t), whole(c2s),
        ],
        out_specs=pl.BlockSpec((1, tq, NSA_HEADS * HEAD_DIM), lambda bi, qi: (bi, qi, 0)),
        out_shape=jax.ShapeDtypeStruct((b, s, NSA_HEADS * HEAD_DIM), BF16),
        scratch_shapes=[
            pltpu.VMEM((g, s, 2 * HEAD_DIM), BF16),
            pltpu.VMEM((g, s, 2 * HEAD_DIM), BF16),
            pltpu.VMEM((g, s + WINDOW, 2 * HEAD_DIM), BF16),
            pltpu.VMEM((g, nc, HEAD_DIM, LANES), BF16),
            pltpu.VMEM((g, nc + WINDOW // LANES, HEAD_DIM, LANES), BF16),
            pltpu.VMEM((tq, tq), F32),
            pltpu.VMEM((tq + WINDOW, tq), F32),
            pltpu.VMEM((g, s, lanes), F32),
            pltpu.VMEM((g, 8, lanes), F32),
            pltpu.VMEM((g, HEAD_DIM, lanes), F32),
        ],
        compiler_params=_params(2),
        name="nsa",
    )(coef, qnt, kc, vct, kv, kv, vt, vt, gnt, gqk, gqk_t, c2s)


def _moba_kernel(coef_ref, qt_ref, k_ref, vt_ref, gqk_ref, gqkt_ref, o_ref,
                 ka_s, kp_s, vt_s, kmean_s, cbias_s, s_scr):
    hp = pl.program_id(1)
    qi = pl.program_id(2)
    tq = MOBA_TQ
    blk = MOBA_BLOCK
    seq = k_ref.shape[2]
    n_blk = seq // blk
    heads = qt_ref.shape[1]

    @pl.when(qi == 0)
    def _():
        with_id = _key_extras(seq, 8, True)
        plain = _key_extras(seq, 8, False)
        for hh in range(heads):
            kn = _rms(k_ref[0, hh], gqk_ref[1:2, :])
            ka_s[hh] = _augment_keys(kn, with_id)
            kp_s[hh] = _augment_keys(kn, plain)
            vt_s[hh] = vt_ref[0, hh].astype(BF16)
            for n in range(n_blk):
                kmean_s[hh, n:n + 1, :] = jnp.mean(kn[n * blk:(n + 1) * blk], axis=0, keepdims=True)
        cbias_s[...] = jnp.where(_iota((blk, tq), 0) <= _iota((blk, tq), 1), 0.0, NEG_INF)

    def tile(n, hh):
        pieces = [coef_ref[hp * heads + hh, i] for i in range(N_PIECES)]
        qnt = _rms_t(qt_ref[0, hh], gqkt_ref[:, 0:1])
        not_sel_t = None
        if n > 0:
            gate_t = _dot_precise(kmean_s[hh], qnt)
            past = _iota((n_blk, tq), 0) < n
            top = _topk_mask_t(jnp.where(past, gate_t, NEG_INF), min(MOBA_TOPK, n_blk))
            not_sel_t = jnp.where(top & past, 0.0, 1.0)
        q_aug = _augment_queries_t(qnt * (SCALE * LOG2E), _query_extras_t(not_sel_t, pieces, tq))
        scr = s_scr.at[hh]
        n_past = n * blk
        st = _dot(kp_s[hh, n_past:n_past + blk, :], q_aug) + cbias_s[...]
        scr[n_past:n_past + blk, :] = st
        m = jnp.max(st, axis=0, keepdims=True)
        if n > 0:
            m = jnp.maximum(m, _scores_pass(q_aug, lambda c0, w: ka_s[hh, c0:c0 + w, :], n_past, scr))
        l, acc = _probs_pass(scr, n_past + blk, m, lambda j: vt_s[hh, j])
        return acc * (1.0 / l)

    for n in range(n_blk):
        @pl.when(qi == n)
        def _(n=n):
            o_t = jnp.concatenate([tile(n, hh) for hh in range(heads)], axis=0)
            o_ref[0] = o_t.T.astype(o_ref.dtype)


def _moba(coef, qt, k, vt, gqk):
    b, h, _, s = qt.shape
    tq = MOBA_TQ
    hpp = 4
    nc = s // LANES
    assert s % MOBA_BLOCK == 0 and h % hpp == 0 and s // MOBA_BLOCK <= X_BLOCKS
    whole = lambda a: pl.BlockSpec(a.shape, lambda bi, hi, qi: (0,) * a.ndim)
    gqk_t = gqk.T
    return pl.pallas_call(
        _moba_kernel,
        grid=(b, h // hpp, s // tq),
        in_specs=[
            pl.BlockSpec(memory_space=pltpu.SMEM),
            pl.BlockSpec((1, hpp, HEAD_DIM, tq), lambda bi, hi, qi: (bi, hi, 0, qi)),
            pl.BlockSpec((1, hpp, s, HEAD_DIM), lambda bi, hi, qi: (bi, hi, 0, 0)),
            pl.BlockSpec((1, hpp, nc, HEAD_DIM, LANES), lambda bi, hi, qi: (bi, hi, 0, 0, 0)),
            whole(gqk), whole(gqk_t),
        ],
        out_specs=pl.BlockSpec((1, tq, hpp * HEAD_DIM), lambda bi, hi, qi: (bi, qi, hi)),
        out_shape=jax.ShapeDtypeStruct((b, s, h * HEAD_DIM), BF16),
        scratch_shapes=[
            pltpu.VMEM((hpp, s, 2 * HEAD_DIM), BF16),
            pltpu.VMEM((hpp, s, 2 * HEAD_DIM), BF16),
            pltpu.VMEM((hpp, nc, HEAD_DIM, LANES), BF16),
            pltpu.VMEM((hpp, s // MOBA_BLOCK, HEAD_DIM), F32),
            pltpu.VMEM((MOBA_BLOCK, tq), F32),
            pltpu.VMEM((hpp, s, tq), F32),
        ],
        compiler_params=_params(3),
        name="moba",
    )(coef, qt, k, vt, gqk, gqk_t)


def _outproj_ffn_kernel(x_ref, on_ref, om_ref, gate_n_ref, gate_m_ref, wun_ref, wum_ref, wo_ref,
                        g_ref, w1_ref, w3_ref, w2_ref, o_ref):
    y = (jax.nn.sigmoid(gate_n_ref[...]) * _dot(on_ref[...], wun_ref[...])
         + jax.nn.sigmoid(gate_m_ref[...]) * _dot(om_ref[...], wum_ref[...]))
    x = x_ref[...] + _dot(y.astype(BF16), wo_ref[...])
    o_ref[...] = _ffn_half_step(x, g_ref, w1_ref, w3_ref, w2_ref)


def _outproj_ffn(x, o_n, o_m, gate_n, gate_m, wun, wum, wo, g, w1, w3, w2):
    n, d = x.shape
    tm = FFN_TM
    row = lambda w: pl.BlockSpec((tm, w), lambda i: (i, 0))
    return pl.pallas_call(
        _outproj_ffn_kernel,
        grid=(n // tm,),
        in_specs=[row(d), row(o_n.shape[1]), row(o_m.shape[1]), row(d), row(d),
                  _const_spec(wun.shape), _const_spec(wum.shape), _const_spec(wo.shape),
                  _const_spec((1, d)), _const_spec(w1.shape), _const_spec(w3.shape), _const_spec(w2.shape)],
        out_specs=row(d),
        out_shape=jax.ShapeDtypeStruct((n, d), F32),
        compiler_params=_params(1),
        name="outproj_ffn",
    )(x, o_n, o_m, gate_n, gate_m, wun, wum, wo, g.reshape(1, d), w1, w3, w2)


def _pack_w_in(w):
    d = w.shape[0]
    col = lambda a, b: w[:, a:b]
    rows = jnp.concatenate(
        [col(512, 640), col(640, 768), col(768, 896), col(1024, 1152),
         col(1816, 2328), col(2840, 3864), col(3864, 4888)], axis=1)
    gn = col(1280, 1304).reshape(d, NSA_KV_HEADS, NSA_GROUP * 3)
    gn = jnp.pad(gn, ((0, 0), (0, 0), (0, _GN_PAD - NSA_GROUP * 3))).reshape(d, NSA_KV_HEADS * _GN_PAD)
    transposed = jnp.concatenate(
        [col(0, 512), col(1304, 1816), col(896, 1024), col(1152, 1280), col(2328, 2840), gn], axis=1).T
    return rows.astype(BF16), transposed.astype(BF16)


def _alibi_coefficients():
    slopes = (2.0 ** (-8.0 * np.arange(1, TOTAL_HEADS + 1) / TOTAL_HEADS)).astype(np.float32)
    rem = slopes.astype(np.float64) * LOG2E
    cols = [slopes]
    for _ in range(N_PIECES):
        piece = rem.astype(BF16).astype(np.float64)
        cols.append(piece.astype(np.float32))
        rem = rem - piece
    table = np.stack(cols, axis=1)
    return jnp.asarray(table[0::2]), jnp.asarray(table[1::2, 1:])


def _cmp_to_slc_t(n_blocks, n_slc):
    ci = np.arange(n_blocks)[None, :] * CMP_STRIDE
    sj = np.arange(n_slc)[:, None] * SLC_BLOCK
    overlap = np.clip(np.minimum(ci + CMP_BLOCK, sj + SLC_BLOCK) - np.maximum(ci, sj), 0, None)
    m = (overlap / CMP_BLOCK).astype(np.float32)
    m[:, n_blocks - 1] = 0.0
    return jnp.asarray(m)


def kernel(x, norm_g, ffn_w1, ffn_w3, ffn_w2, w_in, g_qk_nsa, g_qk_moba, cmp_pos, cmp_w1, cmp_w2,
           w_up_nsa, w_up_moba, w_out):
    b, s, d = x.shape
    depth = norm_g.shape[0]
    coef_nsa, coef_moba = _alibi_coefficients()
    c2s_t = _cmp_to_slc_t(s // CMP_STRIDE, s // SLC_BLOCK)
    xf = x.reshape(b * s, d)
    for l in range(depth):
        xf = _ffn(xf, norm_g[l, 0], ffn_w1[l, 0].astype(BF16), ffn_w3[l, 0].astype(BF16),
                  ffn_w2[l, 0].astype(BF16))
        w_rows, w_t = _pack_w_in(w_in[l])
        kv, km, gate_n, gate_m, qnt, qmt, vt, vmt, gnt = _inproj(
            xf.reshape(b, s, d), norm_g[l, 1], w_rows, w_t)
        kc, vct = _compress(kv, cmp_pos[l], cmp_w1[l].astype(BF16), cmp_w2[l, 0].astype(BF16),
                            cmp_w2[l, 1].T.astype(BF16), g_qk_nsa[l, 1])
        o_n = _nsa(coef_nsa, qnt, kc, vct, kv, vt, gnt, g_qk_nsa[l], c2s_t)
        o_m = _moba(coef_moba, qmt, km, vmt, g_qk_moba[l])
        xf = _outproj_ffn(xf, o_n.reshape(b * s, -1), o_m.reshape(b * s, -1),
                          gate_n.reshape(b * s, -1), gate_m.reshape(b * s, -1),
                          w_up_nsa[l].astype(BF16), w_up_moba[l].astype(BF16), w_out[l].astype(BF16),
                          norm_g[l, 2], ffn_w1[l, 1].astype(BF16), ffn_w3[l, 1].astype(BF16),
                          ffn_w2[l, 1].astype(BF16))
    return xf.reshape(b, s, d)
```

```python
import numpy as np
import jax
import jax.numpy as jnp
from jax import lax
from jax.experimental import pallas as pl
from jax.experimental.pallas import tpu as pltpu

F32 = jnp.float32
BF16 = jnp.bfloat16

HEAD_DIM = 64
NSA_HEADS = 8
NSA_KV_HEADS = 2
NSA_GROUP = NSA_HEADS // NSA_KV_HEADS
CMP_BLOCK = 32
CMP_STRIDE = 16
SLC_BLOCK = 64
SLC_TOPK = 8
WINDOW = 512
MOBA_HEADS = 8
MOBA_BLOCK = 256
MOBA_TOPK = 3
TOTAL_HEADS = NSA_HEADS + MOBA_HEADS
NEG_INF = -1e30
FORCE = 1e9
RMS_EPS = 1e-6
SCALE = HEAD_DIM ** -0.5
LOG2E = 1.4426950408889634

VMEM_LIMIT_BYTES = 56 * 1024 * 1024
LANES = 128

FFN_TM = 512
FFN_FC = 256
NSA_TQ = 128
NSA_KEY_BUCKET = 256
MOBA_TQ = MOBA_BLOCK
KEY_CHUNK = 256

X_BLOCKS = 32
N_PIECES = 4
X_HI = X_BLOCKS
X_LO = X_HI + N_PIECES
X_PAD = X_LO + N_PIECES
MASK_BIG = 2.0 ** 20


def _rms(x, g):
    ms = jnp.mean(x * x, axis=-1, keepdims=True)
    return x * lax.rsqrt(ms + RMS_EPS) * g


def _rms_t(xt, g_col):
    ms = jnp.mean(xt * xt, axis=-2, keepdims=True)
    return xt * lax.rsqrt(ms + RMS_EPS) * g_col


def _dot(a, b):
    return jnp.dot(a, b, preferred_element_type=F32)


def _dot_nt(a, b):
    return lax.dot_general(a, b, (((1,), (1,)), ((), ())), preferred_element_type=F32)


def _split(a):
    hi = a.astype(BF16)
    lo = (a - hi.astype(F32)).astype(BF16)
    return hi, lo


def _dot_precise(a, b):
    ah, al = _split(a)
    bh, bl = _split(b)
    return _dot(ah, bh) + (_dot(ah, bl) + _dot(al, bh))


def _iota(shape, dim):
    return lax.broadcasted_iota(jnp.int32, shape, dim)


def _topk_mask_t(score, k):
    rows = score.shape[0]
    ridx = _iota(score.shape, 0)
    rank = jnp.zeros(score.shape, jnp.int32)
    for j in range(rows):
        sj = score[j:j + 1, :]
        beats = (sj > score) | ((sj == score) & (ridx > j))
        rank = rank + beats.astype(jnp.int32)
    return rank < k


def _key_extras(n_rows, block_shift, with_block_id):
    pos = np.arange(n_rows)[:, None]
    lane = np.arange(HEAD_DIM)[None, :]
    hi = np.broadcast_to((pos >> 6) * 64, (n_rows, HEAD_DIM))
    lo = np.broadcast_to(pos & 63, (n_rows, HEAD_DIM))
    x = np.where((lane >= X_HI) & (lane < X_LO), hi, np.where((lane >= X_LO) & (lane < X_PAD), lo, 0))
    if with_block_id:
        x = np.where((pos >> block_shift) == lane, 1, x)
    return jnp.asarray(x.astype(np.float32))


def _query_extras_t(not_selected_t, pieces, n):
    parts = []
    used = 0
    if not_selected_t is not None:
        parts.append(not_selected_t * (-MASK_BIG))
        used = not_selected_t.shape[0]
    if used < X_BLOCKS:
        parts.append(jnp.zeros((X_BLOCKS - used, n), F32))
    ridx = _iota((2 * N_PIECES, n), 0) & (N_PIECES - 1)
    parts.append(jnp.where(ridx == 0, pieces[0],
                           jnp.where(ridx == 1, pieces[1], jnp.where(ridx == 2, pieces[2], pieces[3]))))
    tail = HEAD_DIM - X_PAD
    parts.append(jnp.where(_iota((tail, n), 0) == 0, 1.0, 0.0))
    return jnp.concatenate(parts, axis=0)


def _augment_queries_t(qt_scaled, extras_t):
    return jnp.concatenate([qt_scaled, extras_t], axis=0).astype(BF16)


def _augment_keys(k_normed, extras):
    return jnp.concatenate([k_normed, extras], axis=1).astype(BF16)


def _scores_pass(q_aug_t, load_keys, n_rows, s_scr):
    m = None
    for c0 in range(0, n_rows, KEY_CHUNK):
        n = min(KEY_CHUNK, n_rows - c0)
        st = _dot(load_keys(c0, n), q_aug_t)
        s_scr[c0:c0 + n, :] = st
        mc = jnp.max(st, axis=0, keepdims=True)
        m = mc if m is None else jnp.maximum(m, mc)
    return m


def _probs_pass(s_scr, n_rows, m, load_vt):
    lanes = s_scr.shape[1]
    l = jnp.zeros((1, lanes), F32)
    acc = jnp.zeros((HEAD_DIM, lanes), F32)
    for c0 in range(0, n_rows, LANES):
        p = jnp.exp2(s_scr[c0:c0 + LANES, :] - m)
        l = l + jnp.sum(p, axis=0, keepdims=True)
        acc = acc + _dot(load_vt(c0 // LANES), p.astype(BF16))
    return l, acc


def _softmax_pv_t(st, vt_chunks):
    m = jnp.max(st, axis=0, keepdims=True)
    p = jnp.exp2(st - m)
    l = jnp.sum(p, axis=0, keepdims=True)
    pb = p.astype(BF16)
    acc = None
    for j, vt in enumerate(vt_chunks):
        d = _dot(vt, pb[j * LANES:(j + 1) * LANES, :])
        acc = d if acc is None else acc + d
    return m, l, acc


def _tile_lanes(x, reps):
    return jnp.concatenate([x] * reps, axis=1)


def _ffn_half_step(x, g_ref, w1_ref, w3_ref, w2_ref):
    hb = _rms(x, g_ref[...]).astype(BF16)
    d_ff = w1_ref.shape[1]
    acc = jnp.zeros(x.shape, F32)
    for c in range(d_ff // FFN_FC):
        sl = slice(c * FFN_FC, (c + 1) * FFN_FC)
        a = _dot(hb, w1_ref[:, sl])
        b = _dot(hb, w3_ref[:, sl])
        u = (a * jax.nn.sigmoid(a) * b).astype(BF16)
        acc = acc + _dot(u, w2_ref[sl, :])
    return x + 0.5 * acc


def _ffn_kernel(x_ref, g_ref, w1_ref, w3_ref, w2_ref, o_ref):
    o_ref[...] = _ffn_half_step(x_ref[...], g_ref, w1_ref, w3_ref, w2_ref)


def _const_spec(shape):
    nd = len(shape)
    return pl.BlockSpec(shape, lambda *_: (0,) * nd, pipeline_mode=pl.Buffered(1))


def _params(n_axes):
    return pltpu.CompilerParams(dimension_semantics=("arbitrary",) * n_axes,
                                vmem_limit_bytes=VMEM_LIMIT_BYTES)


def _ffn(x, g, w1, w3, w2):
    n, d = x.shape
    d_ff = w1.shape[1]
    assert n % FFN_TM == 0 and d_ff % FFN_FC == 0
    return pl.pallas_call(
        _ffn_kernel,
        grid=(n // FFN_TM,),
        in_specs=[
            pl.BlockSpec((FFN_TM, d), lambda i: (i, 0)),
            _const_spec((1, d)),
            _const_spec((d, d_ff)),
            _const_spec((d, d_ff)),
            _const_spec((d_ff, d)),
        ],
        out_specs=pl.BlockSpec((FFN_TM, d), lambda i: (i, 0)),
        out_shape=jax.ShapeDtypeStruct((n, d), F32),
        compiler_params=_params(1),
        name="ffn",
    )(x, g.reshape(1, d), w1, w3, w2)


_ROW_KV, _ROW_KM, _ROW_GATE_N, _ROW_GATE_M, _ROW_COLS = 0, 512, 1024, 2048, 3072
_T_QN, _T_QM, _T_V, _T_VM, _T_GN, _T_ROWS = 0, 512, 1024, 1280, 1792, 1824
_GN_PAD = 16


def _inproj_kernel(x_ref, g_ref, w_ref, wt_ref, kv_ref, km_ref, gate_n_ref, gate_m_ref,
                   qnt_ref, qmt_ref, vt_ref, vmt_ref, gnt_ref):
    hb = _rms(x_ref[0], g_ref[...]).astype(BF16)
    n_chunks = hb.shape[0] // LANES

    def proj(off, width):
        return _dot(hb, w_ref[:, off:off + width])

    p = proj(_ROW_KV, 512)
    for a in range(4):
        for g in range(NSA_KV_HEADS):
            c0 = (a * NSA_KV_HEADS + g) * HEAD_DIM
            kv_ref[0, a, g] = p[:, c0:c0 + HEAD_DIM]
    p = proj(_ROW_KM, 512)
    for h in range(MOBA_HEADS):
        km_ref[0, h] = p[:, h * HEAD_DIM:(h + 1) * HEAD_DIM]
    gate_n_ref[0] = proj(_ROW_GATE_N, 1024)
    gate_m_ref[0] = proj(_ROW_GATE_M, 1024)

    def proj_t(off, height):
        return _dot_nt(wt_ref[off:off + height, :], hb)

    pt = proj_t(_T_QN, 512)
    for h in range(NSA_HEADS):
        qnt_ref[0, h] = pt[h * HEAD_DIM:(h + 1) * HEAD_DIM, :]
    pt = proj_t(_T_QM, 512)
    for h in range(MOBA_HEADS):
        qmt_ref[0, h] = pt[h * HEAD_DIM:(h + 1) * HEAD_DIM, :]
    pt = proj_t(_T_V, 256)
    for j in range(n_chunks):
        for a in range(2):
            for g in range(NSA_KV_HEADS):
                r0 = (a * NSA_KV_HEADS + g) * HEAD_DIM
                vt_ref[0, a, g, j] = pt[r0:r0 + HEAD_DIM, j * LANES:(j + 1) * LANES]
    pt = proj_t(_T_VM, 512)
    for j in range(n_chunks):
        for h in range(MOBA_HEADS):
            vmt_ref[0, h, j] = pt[h * HEAD_DIM:(h + 1) * HEAD_DIM, j * LANES:(j + 1) * LANES]
    pt = proj_t(_T_GN, NSA_KV_HEADS * _GN_PAD)
    for g in range(NSA_KV_HEADS):
        gnt_ref[0, g] = pt[g * _GN_PAD:(g + 1) * _GN_PAD, :]


def _inproj(x, g, w, wt):
    b, s, d = x.shape
    tm = FFN_TM
    assert s % tm == 0 and tm % LANES == 0
    nc, ncb = s // LANES, tm // LANES
    wide = pl.BlockSpec((1, tm, 1024), lambda bi, i: (bi, i, 0))
    qt_spec = pl.BlockSpec((1, NSA_HEADS, HEAD_DIM, tm), lambda bi, i: (bi, 0, 0, i))
    return pl.pallas_call(
        _inproj_kernel,
        grid=(b, s // tm),
        in_specs=[
            pl.BlockSpec((1, tm, d), lambda bi, i: (bi, i, 0)),
            _const_spec((1, d)),
            _const_spec((d, _ROW_COLS)),
            _const_spec((_T_ROWS, d)),
        ],
        out_specs=[
            pl.BlockSpec((1, 4, NSA_KV_HEADS, tm, HEAD_DIM), lambda bi, i: (bi, 0, 0, i, 0)),
            pl.BlockSpec((1, MOBA_HEADS, tm, HEAD_DIM), lambda bi, i: (bi, 0, i, 0)),
            wide, wide,
            qt_spec, qt_spec,
            pl.BlockSpec((1, 2, NSA_KV_HEADS, ncb, HEAD_DIM, LANES), lambda bi, i: (bi, 0, 0, i, 0, 0)),
            pl.BlockSpec((1, MOBA_HEADS, ncb, HEAD_DIM, LANES), lambda bi, i: (bi, 0, i, 0, 0)),
            pl.BlockSpec((1, NSA_KV_HEADS, _GN_PAD, tm), lambda bi, i: (bi, 0, 0, i)),
        ],
        out_shape=[
            jax.ShapeDtypeStruct((b, 4, NSA_KV_HEADS, s, HEAD_DIM), F32),
            jax.ShapeDtypeStruct((b, MOBA_HEADS, s, HEAD_DIM), F32),
            jax.ShapeDtypeStruct((b, s, 1024), F32),
            jax.ShapeDtypeStruct((b, s, 1024), F32),
            jax.ShapeDtypeStruct((b, NSA_HEADS, HEAD_DIM, s), F32),
            jax.ShapeDtypeStruct((b, MOBA_HEADS, HEAD_DIM, s), F32),
            jax.ShapeDtypeStruct((b, 2, NSA_KV_HEADS, nc, HEAD_DIM, LANES), F32),
            jax.ShapeDtypeStruct((b, MOBA_HEADS, nc, HEAD_DIM, LANES), F32),
            jax.ShapeDtypeStruct((b, NSA_KV_HEADS, _GN_PAD, s), F32),
        ],
        compiler_params=_params(2),
        name="inproj",
    )(x, g.reshape(1, d), w, wt)


def _compress_kernel(kc_ref, vc_ref, pos_ref, w1_ref, w2k_ref, w2vt_ref, gk_ref, kco_ref, vcto_ref):
    n_blocks = kc_ref.shape[3] // CMP_STRIDE
    for a, src in enumerate((kc_ref, vc_ref)):
        first = jnp.zeros((n_blocks, w1_ref.shape[2]), F32)
        second = jnp.zeros((n_blocks, w1_ref.shape[2]), F32)
        for p in range(CMP_STRIDE):
            tok = src[0, 0, 0, pl.ds(p, n_blocks, stride=CMP_STRIDE), :]
            q = p + CMP_STRIDE
            first = first + _dot((tok + pos_ref[a, p:p + 1, :]).astype(BF16),
                                 w1_ref[a, p * HEAD_DIM:(p + 1) * HEAD_DIM, :])
            second = second + _dot((tok + pos_ref[a, q:q + 1, :]).astype(BF16),
                                   w1_ref[a, q * HEAD_DIM:(q + 1) * HEAD_DIM, :])
        hid = first + pltpu.roll(second, n_blocks - 1, 0)
        act = jax.nn.gelu(hid).astype(BF16)
        if a == 0:
            y = _rms(_dot(act, w2k_ref[...]), gk_ref[...])
            kco_ref[0, 0] = jnp.where(_iota(y.shape, 0) < n_blocks - 1, y, 0.0)
        else:
            yt = _dot_nt(w2vt_ref[...], act)
            vcto_ref[0, 0] = jnp.where(_iota(yt.shape, 1) < n_blocks - 1, yt, 0.0)


def _compress(kv, pos, w1, w2k, w2vt, gk):
    b, _, g, s, _ = kv.shape
    n_blocks = s // CMP_STRIDE
    src = lambda a: pl.BlockSpec((1, 1, 1, s, HEAD_DIM), lambda bi, gi: (bi, a, gi, 0, 0))
    return pl.pallas_call(
        _compress_kernel,
        grid=(b, g),
        in_specs=[src(0), src(1), _const_spec(pos.shape), _const_spec(w1.shape),
                  _const_spec(w2k.shape), _const_spec(w2vt.shape), _const_spec((1, HEAD_DIM))],
        out_specs=[pl.BlockSpec((1, 1, n_blocks, HEAD_DIM), lambda bi, gi: (bi, gi, 0, 0)),
                   pl.BlockSpec((1, 1, HEAD_DIM, n_blocks), lambda bi, gi: (bi, gi, 0, 0))],
        out_shape=[jax.ShapeDtypeStruct((b, g, n_blocks, HEAD_DIM), F32),
                   jax.ShapeDtypeStruct((b, g, HEAD_DIM, n_blocks), F32)],
        compiler_params=_params(2),
        name="compress",
    )(kv, kv, pos, w1, w2k, w2vt, gk.reshape(1, HEAD_DIM))


def _nsa_kernel(coef_ref, qt_ref, kc_ref, vct_ref, ks_ref, kw_ref, vst_ref, vwt_ref,
                gnt_ref, gqk_ref, gqkt_ref, c2s_ref, xid_ref, xplain_ref, o_ref,
                ksa_s, ksp_s, kwp_s, vst_s, vwt_s, cbias_s, wbias_s, s_scr, ml_s, acc_s):
    qi = pl.program_id(1)
    tq = NSA_TQ
    r_heads = NSA_GROUP
    groups = NSA_KV_HEADS
    lanes = r_heads * tq
    seq = ks_ref.shape[3]
    wk = tq + WINDOW
    n_wpad = WINDOW // LANES

    @pl.when(qi == 0)
    def _():
        plain = xplain_ref[...]
        with_id = xid_ref[...]
        pad_rows = jnp.where(_iota((WINDOW, 2 * HEAD_DIM), 1) == HEAD_DIM + X_PAD, -MASK_BIG, 0.0).astype(BF16)
        for g in range(groups):
            kn = _rms(ks_ref[0, 0, g], gqk_ref[2:3, :])
            ksa_s[g] = _augment_keys(kn, with_id)
            ksp_s[g] = _augment_keys(kn, plain)
            kwp_s[g, 0:WINDOW, :] = pad_rows
            kwp_s[g, WINDOW:, :] = _augment_keys(_rms(kw_ref[0, 0, g], gqk_ref[3:4, :]), plain)
            vst_s[g] = vst_ref[0, 0, g].astype(BF16)
            vwt_s[g, 0:n_wpad] = jnp.zeros((n_wpad, HEAD_DIM, LANES), BF16)
            vwt_s[g, n_wpad:] = vwt_ref[0, 0, g].astype(BF16)
        cbias_s[...] = jnp.where(_iota((tq, tq), 0) <= _iota((tq, tq), 1), 0.0, NEG_INF)
        wr, wi = _iota((wk, tq), 0), _iota((wk, tq), 1)
        wbias_s[...] = jnp.where((wr > wi) & (wr <= WINDOW + wi), 0.0, NEG_INF)

    start = qi * tq
    own = pl.multiple_of(start, tq)

    def lanes_of(x, r):
        return x[:, r * tq:(r + 1) * tq]

    def before_buckets(g):
        head = lambda r: g * r_heads + r
        pieces = lambda r: [coef_ref[head(r), 1 + i] for i in range(N_PIECES)]
        qnt4 = _rms_t(qt_ref[0, g * r_heads:(g + 1) * r_heads], gqkt_ref[:, 0:1])
        qnt = jnp.concatenate([qnt4[r] for r in range(r_heads)], axis=1)
        q_log2 = qnt * (SCALE * LOG2E)

        q_plain = _augment_queries_t(
            q_log2, jnp.concatenate([_query_extras_t(None, pieces(r), tq) for r in range(r_heads)], axis=1))
        st = _dot(kwp_s[g, pl.ds(own, wk), :], q_plain) + _tile_lanes(wbias_s[...], r_heads)
        _, l_w, acc_w = _softmax_pv_t(st, [vwt_s[g, qi + j] for j in range(wk // LANES)])
        o_win_t = acc_w * (1.0 / l_w)
        st = _dot(ksp_s[g, pl.ds(own, tq), :], q_plain) + _tile_lanes(cbias_s[...], r_heads)
        own_part = _softmax_pv_t(st, [vst_s[g, qi]])

        kc = kc_ref[0, g]
        vct = vct_ref[0, g].astype(BF16)
        n_c = kc.shape[0]
        sc_t = _dot_precise(kc, qnt * SCALE)
        t_lane = start + _iota((n_c, tq), 1)
        c_start = _iota((n_c, tq), 0) * CMP_STRIDE
        c_valid = (c_start + (CMP_BLOCK - 1)) <= t_lane
        c_dist = t_lane.astype(F32) - (c_start.astype(F32) + (CMP_BLOCK - 1) / 2)
        o_cmp_t = []
        p_sum = jnp.zeros((n_c, tq), F32)
        for r in range(r_heads):
            sm = jnp.where(c_valid, lanes_of(sc_t, r) - coef_ref[head(r), 0] * c_dist, NEG_INF)
            m = jnp.max(sm, axis=0, keepdims=True)
            p = jnp.where(c_valid, jnp.exp(sm - m), 0.0)
            l = jnp.sum(p, axis=0, keepdims=True)
            p = p * jnp.where(l > 0.0, 1.0 / l, 0.0)
            p_sum = p_sum + p
            o_cmp_t.append(_dot(vct, p.astype(BF16)))

        n_slc = c2s_ref.shape[0]
        ph, pl_ = _split(p_sum)
        c2s = c2s_ref[...].astype(BF16)
        imp_t = _dot(c2s, ph) + _dot(c2s, pl_)
        blk = _iota((n_slc, tq), 0)
        cur = lax.shift_right_logical(start + _iota((n_slc, tq), 1), 6)
        forced = (blk == 0) | (blk == cur) | (blk == cur - 1)
        sel_score = jnp.where(forced, FORCE, jnp.where(blk <= cur, imp_t, NEG_INF))
        sel = _topk_mask_t(sel_score, min(SLC_TOPK, n_slc))
        first_own = lax.shift_right_logical(start, 6)
        not_sel_t = jnp.where(sel & (blk < first_own), 0.0, 1.0)
        q_sel = _augment_queries_t(
            q_log2,
            jnp.concatenate([_query_extras_t(not_sel_t, pieces(r), tq) for r in range(r_heads)], axis=1))
        return q_sel, own_part, o_win_t, o_cmp_t

    state = [before_buckets(g) for g in range(groups)]

    for g in range(groups):
        ml_s[g, 0:1, :] = jnp.full((1, lanes), NEG_INF, F32)
        ml_s[g, 1:2, :] = jnp.zeros((1, lanes), F32)
        acc_s[g] = jnp.zeros((HEAD_DIM, lanes), F32)
    per_bucket = NSA_KEY_BUCKET // tq
    for bk in range(1, seq // NSA_KEY_BUCKET + 1):
        @pl.when((qi + per_bucket - 1) // per_bucket == bk)
        def _(bk=bk):
            n_keys = bk * NSA_KEY_BUCKET
            ms = [_scores_pass(state[g][0], lambda c0, n, g=g: ksa_s[g, c0:c0 + n, :], n_keys, s_scr.at[g])
                  for g in range(groups)]
            for g in range(groups):
                l, acc = _probs_pass(s_scr.at[g], n_keys, ms[g], lambda j, g=g: vst_s[g, j])
                ml_s[g, 0:1, :] = ms[g]
                ml_s[g, 1:2, :] = l
                acc_s[g] = acc

    outs = []
    for g in range(groups):
        _, (m2, l2, acc2), o_win_t, o_cmp_t = state[g]
        m1, l1 = ml_s[g, 0:1, :], ml_s[g, 1:2, :]
        m = jnp.maximum(m1, m2)
        w1, w2 = jnp.exp2(m1 - m), jnp.exp2(m2 - m)
        o_slc_t = (acc_s[g] * w1 + acc2 * w2) * (1.0 / (l1 * w1 + l2 * w2))
        gate = jax.nn.sigmoid(gnt_ref[0, g])
        for r in range(r_heads):
            outs.append(gate[3 * r:3 * r + 1, :] * o_cmp_t[r]
                        + gate[3 * r + 1:3 * r + 2, :] * lanes_of(o_slc_t, r)
                        + gate[3 * r + 2:3 * r + 3, :] * lanes_of(o_win_t, r))
    o_ref[0] = jnp.concatenate(outs, axis=0).T.astype(o_ref.dtype)


def _nsa(coef, qnt, kc, vct, kv, vt, gnt, gqk, c2s):
    b, _, _, s = qnt.shape
    g = NSA_KV_HEADS
    tq = NSA_TQ
    n_c = kc.shape[2]
    nc = s // LANES
    lanes = NSA_GROUP * tq
    assert s % NSA_KEY_BUCKET == 0 and NSA_KEY_BUCKET % tq == 0 and WINDOW % LANES == 0
    assert tq == LANES and s // SLC_BLOCK <= X_BLOCKS
    kspec = lambda a: pl.BlockSpec((1, 1, g, s, HEAD_DIM), lambda bi, qi: (bi, a, 0, 0, 0))
    vspec = lambda a: pl.BlockSpec((1, 1, g, nc, HEAD_DIM, LANES), lambda bi, qi: (bi, a, 0, 0, 0, 0))
    whole = lambda a: pl.BlockSpec(a.shape, lambda bi, qi: (0,) * a.ndim)
    gqk_t = gqk.T
    x_id, x_plain = _key_extras(s, 6, True), _key_extras(s, 6, False)
    return pl.pallas_call(
        _nsa_kernel,
        grid=(b, s // tq),
        in_specs=[
            pl.BlockSpec(memory_space=pltpu.SMEM),
            pl.BlockSpec((1, NSA_HEADS, HEAD_DIM, tq), lambda bi, qi: (bi, 0, 0, qi)),
            pl.BlockSpec((1, g, n_c, HEAD_DIM), lambda bi, qi: (bi, 0, 0, 0)),
            pl.BlockSpec((1, g, HEAD_DIM, n_c), lambda bi, qi: (bi, 0, 0, 0)),
            kspec(2), kspec(3), vspec(0), vspec(1),
            pl.BlockSpec((1, g, _GN_PAD, tq), lambda bi, qi: (bi, 0, 0, qi)),
            whole(gqk), whole(gqk_t), whole(c2s), whole(x_id), whole(x_plain),
        ],
        out_specs=pl.BlockSpec((1, tq, NSA_HEADS * HEAD_DIM), lambda bi, qi: (bi, qi, 0)),
        out_shape=jax.ShapeDtypeStruct((b, s, NSA_HEADS * HEAD_DIM), BF16),
        scratch_shapes=[
            pltpu.VMEM((g, s, 2 * HEAD_DIM), BF16),
            pltpu.VMEM((g, s, 2 * HEAD_DIM), BF16),
            pltpu.VMEM((g, s + WINDOW, 2 * HEAD_DIM), BF16),
            pltpu.VMEM((g, nc, HEAD_DIM, LANES), BF16),
            pltpu.VMEM((g, nc + WINDOW // LANES, HEAD_DIM, LANES), BF16),
            pltpu.VMEM((tq, tq), F32),
            pltpu.VMEM((tq + WINDOW, tq), F32),
            pltpu.VMEM((g, s, lanes), F32),
            pltpu.VMEM((g, 8, lanes), F32),
            pltpu.VMEM((g, HEAD_DIM, lanes), F32),
        ],
        compiler_params=_params(2),
        name="nsa",
    )(coef, qnt, kc, vct, kv, kv, vt, vt, gnt, gqk, gqk_t, c2s, x_id, x_plain)


def _moba_kernel(coef_ref, qt_ref, k_ref, vt_ref, gqk_ref, gqkt_ref, xid_ref, xplain_ref, o_ref,
                 ka_s, kp_s, vt_s, kmean_s, cbias_s, s_scr):
    hp = pl.program_id(1)
    qi = pl.program_id(2)
    tq = MOBA_TQ
    blk = MOBA_BLOCK
    seq = k_ref.shape[2]
    n_blk = seq // blk
    heads = qt_ref.shape[1]

    @pl.when(qi == 0)
    def _():
        with_id = xid_ref[...]
        plain = xplain_ref[...]
        for hh in range(heads):
            kn = _rms(k_ref[0, hh], gqk_ref[1:2, :])
            ka_s[hh] = _augment_keys(kn, with_id)
            kp_s[hh] = _augment_keys(kn, plain)
            vt_s[hh] = vt_ref[0, hh].astype(BF16)
            for n in range(n_blk):
                kmean_s[hh, n:n + 1, :] = jnp.mean(kn[n * blk:(n + 1) * blk], axis=0, keepdims=True)
        cbias_s[...] = jnp.where(_iota((blk, tq), 0) <= _iota((blk, tq), 1), 0.0, NEG_INF)

    def tile(n, hh):
        pieces = [coef_ref[hp * heads + hh, i] for i in range(N_PIECES)]
        qnt = _rms_t(qt_ref[0, hh], gqkt_ref[:, 0:1])
        not_sel_t = None
        if n > 0:
            gate_t = _dot_precise(kmean_s[hh], qnt)
            past = _iota((n_blk, tq), 0) < n
            top = _topk_mask_t(jnp.where(past, gate_t, NEG_INF), min(MOBA_TOPK, n_blk))
            not_sel_t = jnp.where(top & past, 0.0, 1.0)
        q_aug = _augment_queries_t(qnt * (SCALE * LOG2E), _query_extras_t(not_sel_t, pieces, tq))
        scr = s_scr.at[hh]
        n_past = n * blk
        st = _dot(kp_s[hh, n_past:n_past + blk, :], q_aug) + cbias_s[...]
        scr[n_past:n_past + blk, :] = st
        m = jnp.max(st, axis=0, keepdims=True)
        if n > 0:
            m = jnp.maximum(m, _scores_pass(q_aug, lambda c0, w: ka_s[hh, c0:c0 + w, :], n_past, scr))
        l, acc = _probs_pass(scr, n_past + blk, m, lambda j: vt_s[hh, j])
        return acc * (1.0 / l)

    for n in range(n_blk):
        @pl.when(qi == n)
        def _(n=n):
            o_t = jnp.concatenate([tile(n, hh) for hh in range(heads)], axis=0)
            o_ref[0] = o_t.T.astype(o_ref.dtype)


def _moba(coef, qt, k, vt, gqk):
    b, h, _, s = qt.shape
    tq = MOBA_TQ
    hpp = 4
    nc = s // LANES
    assert s % MOBA_BLOCK == 0 and h % hpp == 0 and s // MOBA_BLOCK <= X_BLOCKS
    whole = lambda a: pl.BlockSpec(a.shape, lambda bi, hi, qi: (0,) * a.ndim)
    gqk_t = gqk.T
    x_id, x_plain = _key_extras(s, 8, True), _key_extras(s, 8, False)
    return pl.pallas_call(
        _moba_kernel,
        grid=(b, h // hpp, s // tq),
        in_specs=[
            pl.BlockSpec(memory_space=pltpu.SMEM),
            pl.BlockSpec((1, hpp, HEAD_DIM, tq), lambda bi, hi, qi: (bi, hi, 0, qi)),
            pl.BlockSpec((1, hpp, s, HEAD_DIM), lambda bi, hi, qi: (bi, hi, 0, 0)),
            pl.BlockSpec((1, hpp, nc, HEAD_DIM, LANES), lambda bi, hi, qi: (bi, hi, 0, 0, 0)),
            whole(gqk), whole(gqk_t), whole(x_id), whole(x_plain),
        ],
        out_specs=pl.BlockSpec((1, tq, hpp * HEAD_DIM), lambda bi, hi, qi: (bi, qi, hi)),
        out_shape=jax.ShapeDtypeStruct((b, s, h * HEAD_DIM), BF16),
        scratch_shapes=[
            pltpu.VMEM((hpp, s, 2 * HEAD_DIM), BF16),
            pltpu.VMEM((hpp, s, 2 * HEAD_DIM), BF16),
            pltpu.VMEM((hpp, nc, HEAD_DIM, LANES), BF16),
            pltpu.VMEM((hpp, s // MOBA_BLOCK, HEAD_DIM), F32),
            pltpu.VMEM((MOBA_BLOCK, tq), F32),
            pltpu.VMEM((hpp, s, tq), F32),
        ],
        compiler_params=_params(3),
        name="moba",
    )(coef, qt, k, vt, gqk, gqk_t, x_id, x_plain)


def _outproj_ffn_kernel(x_ref, on_ref, om_ref, gate_n_ref, gate_m_ref, wun_ref, wum_ref, wo_ref,
                        g_ref, w1_ref, w3_ref, w2_ref, o_ref):
    y = (jax.nn.sigmoid(gate_n_ref[...]) * _dot(on_ref[...], wun_ref[...])
         + jax.nn.sigmoid(gate_m_ref[...]) * _dot(om_ref[...], wum_ref[...]))
    x = x_ref[...] + _dot(y.astype(BF16), wo_ref[...])
    o_ref[...] = _ffn_half_step(x, g_ref, w1_ref, w3_ref, w2_ref)


def _outproj_ffn(x, o_n, o_m, gate_n, gate_m, wun, wum, wo, g, w1, w3, w2):
    n, d = x.shape
    tm = FFN_TM
    row = lambda w: pl.BlockSpec((tm, w), lambda i: (i, 0))
    return pl.pallas_call(
        _outproj_ffn_kernel,
        grid=(n // tm,),
        in_specs=[row(d), row(o_n.shape[1]), row(o_m.shape[1]), row(d), row(d),
                  _const_spec(wun.shape), _const_spec(wum.shape), _const_spec(wo.shape),
                  _const_spec((1, d)), _const_spec(w1.shape), _const_spec(w3.shape), _const_spec(w2.shape)],
        out_specs=row(d),
        out_shape=jax.ShapeDtypeStruct((n, d), F32),
        compiler_params=_params(1),
        name="outproj_ffn",
    )(x, o_n, o_m, gate_n, gate_m, wun, wum, wo, g.reshape(1, d), w1, w3, w2)


def _pack_w_in(w):
    d = w.shape[0]
    col = lambda a, b: w[:, a:b]
    rows = jnp.concatenate(
        [col(512, 640), col(640, 768), col(768, 896), col(1024, 1152),
         col(1816, 2328), col(2840, 3864), col(3864, 4888)], axis=1)
    gn = col(1280, 1304).reshape(d, NSA_KV_HEADS, NSA_GROUP * 3)
    gn = jnp.pad(gn, ((0, 0), (0, 0), (0, _GN_PAD - NSA_GROUP * 3))).reshape(d, NSA_KV_HEADS * _GN_PAD)
    transposed = jnp.concatenate(
        [col(0, 512), col(1304, 1816), col(896, 1024), col(1152, 1280), col(2328, 2840), gn], axis=1).T
    return rows.astype(BF16), transposed.astype(BF16)


def _alibi_coefficients():
    slopes = (2.0 ** (-8.0 * np.arange(1, TOTAL_HEADS + 1) / TOTAL_HEADS)).astype(np.float32)
    rem = slopes.astype(np.float64) * LOG2E
    cols = [slopes]
    for _ in range(N_PIECES):
        piece = rem.astype(BF16).astype(np.float64)
        cols.append(piece.astype(np.float32))
        rem = rem - piece
    table = np.stack(cols, axis=1)
    return jnp.asarray(table[0::2]), jnp.asarray(table[1::2, 1:])


def _cmp_to_slc_t(n_blocks, n_slc):
    ci = np.arange(n_blocks)[None, :] * CMP_STRIDE
    sj = np.arange(n_slc)[:, None] * SLC_BLOCK
    overlap = np.clip(np.minimum(ci + CMP_BLOCK, sj + SLC_BLOCK) - np.maximum(ci, sj), 0, None)
    m = (overlap / CMP_BLOCK).astype(np.float32)
    m[:, n_blocks - 1] = 0.0
    return jnp.asarray(m)


def kernel(x, norm_g, ffn_w1, ffn_w3, ffn_w2, w_in, g_qk_nsa, g_qk_moba, cmp_pos, cmp_w1, cmp_w2,
           w_up_nsa, w_up_moba, w_out):
    b, s, d = x.shape
    depth = norm_g.shape[0]
    coef_nsa, coef_moba = _alibi_coefficients()
    c2s_t = _cmp_to_slc_t(s // CMP_STRIDE, s // SLC_BLOCK)
    xf = x.reshape(b * s, d)
    for l in range(depth):
        xf = _ffn(xf, norm_g[l, 0], ffn_w1[l, 0].astype(BF16), ffn_w3[l, 0].astype(BF16),
                  ffn_w2[l, 0].astype(BF16))
        w_rows, w_t = _pack_w_in(w_in[l])
        kv, km, gate_n, gate_m, qnt, qmt, vt, vmt, gnt = _inproj(
            xf.reshape(b, s, d), norm_g[l, 1], w_rows, w_t)
        kc, vct = _compress(kv, cmp_pos[l], cmp_w1[l].astype(BF16), cmp_w2[l, 0].astype(BF16),
                            cmp_w2[l, 1].T.astype(BF16), g_qk_nsa[l, 1])
        o_n = _nsa(coef_nsa, qnt, kc, vct, kv, vt, gnt, g_qk_nsa[l], c2s_t)
        o_m = _moba(coef_moba, qmt, km, vmt, g_qk_moba[l])
        xf = _outproj_ffn(xf, o_n.reshape(b * s, -1), o_m.reshape(b * s, -1),
                          gate_n.reshape(b * s, -1), gate_m.reshape(b * s, -1),
                          w_up_nsa[l].astype(BF16), w_up_moba[l].astype(BF16), w_out[l].astype(BF16),
                          norm_g[l, 2], ffn_w1[l, 1].astype(BF16), ffn_w3[l, 1].astype(BF16),
                          ffn_w2[l, 1].astype(BF16))
    return xf.reshape(b, s, d)
```
